```python
import math
import jax, jax.numpy as jnp
from jax import lax
import numpy as np

D_MODEL = 1024
BATCH = 32
SEQ = 2048
DEPTH = 4

NSA_HEADS = 8
NSA_KV_GROUPS = 2
NSA_HEAD_DIM = 64
NSA_REP = NSA_HEADS // NSA_KV_GROUPS
NSA_WIDTH = NSA_HEADS * NSA_HEAD_DIM
KV_WIDTH = NSA_KV_GROUPS * NSA_HEAD_DIM
CMP_BLOCK = 32
CMP_STRIDE = 16
CMP_HIDDEN = 128
SLC_BLOCK = 64
SLC_TOP = 16
SLC_QBLOCK = 16
WIN = 512
WIN_QBLOCK = 128
SSD_HEADS = 8
SSD_HEAD_DIM = 64
SSD_WIDTH = SSD_HEADS * SSD_HEAD_DIM
SSD_GROUPS = 2
SSD_STATE = 128
SSD_CONV = 4
SSD_CHUNK = 256
SSD_XBC = SSD_WIDTH + 2 * SSD_GROUPS * SSD_STATE
D_MIX = NSA_WIDTH + SSD_WIDTH
D_FF = 2816
FFN_CONV = 3
RMS_EPS = 1e-6
NEG = -1e30
FORCE_BONUS = 1e4

IN_SIZES = [NSA_WIDTH, KV_WIDTH, KV_WIDTH, KV_WIDTH, KV_WIDTH, KV_WIDTH, KV_WIDTH,
            3 * NSA_HEADS, SSD_WIDTH, SSD_XBC, SSD_HEADS]
IN_COLS = sum(IN_SIZES)
IN_SPLITS = [int(v) for v in np.cumsum(IN_SIZES)[:-1]]

kernel_name = "hymba_nsa_ssd_convffn"


def _rmsnorm(x, w):
    xf = x.astype(jnp.float32)
    y = xf * lax.rsqrt(jnp.mean(xf * xf, axis=-1, keepdims=True) + RMS_EPS)
    return (y * w).astype(x.dtype)


def _causal_dwconv(x, w, b):
    k, c = w.shape
    y = lax.conv_general_dilated(x, w[:, None, :], window_strides=(1,), padding=[(k - 1, 0)],
                                 dimension_numbers=('NWC', 'WIO', 'NWC'), feature_group_count=c)
    return y + b


def _masked_softmax(s, mask):
    p = jax.nn.softmax(jnp.where(mask, s, NEG), axis=-1)
    return jnp.where(mask, p, 0.0)


def _compress(k, pos, w1, b1, w2):
    b, s, g, d = k.shape
    nc = (s - CMP_BLOCK) // CMP_STRIDE + 1
    idx = np.arange(nc)[:, None] * CMP_STRIDE + np.arange(CMP_BLOCK)[None, :]
    kb = k[:, idx] + pos[:, None, :]
    kb = jnp.moveaxis(kb, 2, 3).reshape(b, nc, g, CMP_BLOCK * d)
    return jax.nn.gelu(kb @ w1 + b1) @ w2


def _select_blocks(p_cmp, s):
    nc = p_cmp.shape[-1]
    ns = s // SLC_BLOCK
    c0 = np.arange(nc)[:, None] * CMP_STRIDE
    j0 = np.arange(ns)[None, :] * SLC_BLOCK
    overlap = jnp.asarray(((c0 < j0 + SLC_BLOCK) & (c0 + CMP_BLOCK > j0)).astype(np.float32))
    imp = jnp.einsum('bgrsc,cj->bgsj', p_cmp, overlap)
    cur = np.arange(s)[:, None] // SLC_BLOCK
    jb = np.arange(ns)[None, :]
    valid = jb <= cur
    forced = ((jb == 0) | (jb == cur) | (jb == cur - 1)).astype(np.float32)
    score = jnp.where(valid, imp + FORCE_BONUS * forced, NEG)
    _, idx = lax.top_k(score, min(SLC_TOP, ns))
    return idx


def _selected_attn(q, k, v, blk_idx):
    b, s, g, r, d = q.shape
    ns = s // SLC_BLOCK
    n = blk_idx.shape[-1]
    nq = s // SLC_QBLOCK
    scale = d ** -0.5
    kb = k.reshape(b, ns, SLC_BLOCK, g, d).transpose(0, 3, 1, 2, 4)
    vb = v.reshape(b, ns, SLC_BLOCK, g, d).transpose(0, 3, 1, 2, 4)
    qs = jnp.moveaxis(q.reshape(b, nq, SLC_QBLOCK, g, r, d), 1, 0)
    ixs = jnp.moveaxis(blk_idx.reshape(b, g, nq, SLC_QBLOCK, n), 2, 0)
    ts = jnp.arange(s).reshape(nq, SLC_QBLOCK)
    gather = jax.vmap(jax.vmap(lambda blocks, ix: blocks[ix]))

    def block(args):
        q_i, ix, t = args
        k_i = gather(kb, ix)
        v_i = gather(vb, ix)
        sc = jnp.einsum('bqgrd,bgqnld->bgrqnl', q_i, k_i).astype(jnp.float32) * scale
        key_pos = ix[..., None] * SLC_BLOCK + jnp.arange(SLC_BLOCK)
        mask = (key_pos <= t[:, None, None])[:, :, None]
        p = _masked_softmax(sc.reshape(b, g, r, SLC_QBLOCK, n * SLC_BLOCK),
                            mask.reshape(b, g, 1, SLC_QBLOCK, n * SLC_BLOCK))
        p = p.reshape(b, g, r, SLC_QBLOCK, n, SLC_BLOCK).astype(v.dtype)
        return jnp.einsum('bgrqnl,bgqnld->bqgrd', p, v_i)

    o = lax.map(block, (qs, ixs, ts))
    return jnp.moveaxis(o, 0, 1).reshape(b, s, g, r, d)


def _window_attn(q, k, v):
    b, s, g, r, d = q.shape
    nq = s // WIN_QBLOCK
    span = WIN_QBLOCK + WIN
    scale = d ** -0.5
    kp = jnp.pad(k, ((0, 0), (WIN, 0), (0, 0), (0, 0)))
    vp = jnp.pad(v, ((0, 0), (WIN, 0), (0, 0), (0, 0)))
    rel = np.arange(WIN_QBLOCK)[:, None] + WIN - np.arange(span)[None, :]
    band = (rel >= 0) & (rel < WIN)

    def block(i):
        q_i = lax.dynamic_slice_in_dim(q, i * WIN_QBLOCK, WIN_QBLOCK, axis=1)
        k_i = lax.dynamic_slice_in_dim(kp, i * WIN_QBLOCK, span, axis=1)
        v_i = lax.dynamic_slice_in_dim(vp, i * WIN_QBLOCK, span, axis=1)
        key_pos = i * WIN_QBLOCK - WIN + jnp.arange(span)
        mask = band & (key_pos >= 0)[None, :]
        sc = jnp.einsum('bqgrd,bkgd->bgrqk', q_i, k_i).astype(jnp.float32) * scale
        p = _masked_softmax(sc, mask).astype(v.dtype)
        return jnp.einsum('bgrqk,bkgd->bqgrd', p, v_i)

    o = lax.map(block, jnp.arange(nq))
    return jnp.moveaxis(o, 0, 1).reshape(b, s, g, r, d)


def _nsa(q, kc, vc, ks, vs, kw, vw, gl, pos_k, w1_k, b1_k, w2_k, pos_v, w1_v, b1_v, w2_v, out_norm_w):
    b, s, _ = q.shape
    G, R, dk = NSA_KV_GROUPS, NSA_REP, NSA_HEAD_DIM
    q = q.reshape(b, s, G, R, dk)
    kv = lambda t: t.reshape(b, s, G, dk)
    k_cmp = _compress(kv(kc), pos_k, w1_k, b1_k, w2_k)
    v_cmp = _compress(kv(vc), pos_v, w1_v, b1_v, w2_v)
    nc = k_cmp.shape[1]
    sc = jnp.einsum('bsgrd,bcgd->bgrsc', q, k_cmp).astype(jnp.float32) * dk ** -0.5
    cmp_end = np.arange(nc) * CMP_STRIDE + CMP_BLOCK - 1
    cmask = cmp_end[None, :] <= np.arange(s)[:, None]
    p_cmp = _masked_softmax(sc, cmask)
    o_cmp = jnp.einsum('bgrsc,bcgd->bsgrd', p_cmp.astype(v_cmp.dtype), v_cmp)
    idx = _select_blocks(p_cmp, s)
    o_slc = _selected_attn(q, kv(ks), kv(vs), idx)
    o_win = _window_attn(q, kv(kw), kv(vw))
    gates = jax.nn.sigmoid(gl.astype(jnp.float32)).reshape(b, s, G, R, 3).astype(q.dtype)
    o = gates[..., 0:1] * o_cmp + gates[..., 1:2] * o_slc + gates[..., 2:3] * o_win
    return _rmsnorm(o.reshape(b, s, NSA_WIDTH), out_norm_w)


def _ssd_scan(x, dt, A, Bm, Cm, chunk):
    b, s, h, p = x.shape
    g, n = Bm.shape[2], Bm.shape[3]
    r = h // g
    c = s // chunk
    xf = (x * dt[..., None]).reshape(b, c, chunk, g, r, p)
    a = jnp.moveaxis((dt * A).reshape(b, c, chunk, g, r), 2, -1)
    a_cs = jnp.cumsum(a, axis=-1)
    Bc = Bm.reshape(b, c, chunk, g, n)
    Cc = Cm.reshape(b, c, chunk, g, n)
    tri = np.tril(np.ones((chunk, chunk), dtype=bool))
    seg = a_cs[..., :, None] - a_cs[..., None, :]
    Lmat = jnp.where(tri, jnp.exp(jnp.where(tri, seg, 0.0)), 0.0)
    cb = jnp.einsum('bclgn,bcsgn->bcgls', Cc, Bc)
    y_diag = jnp.einsum('bcgrls,bcsgrp->bclgrp', cb[:, :, :, None] * Lmat, xf)
    decay = jnp.exp(a_cs[..., -1:] - a_cs)
    states = jnp.einsum('bclgn,bcgrl,bclgrp->bcgrpn', Bc, decay, xf)
    chunk_decay = jnp.exp(a_cs[..., -1])

    def step(carry, inp):
        st, dec = inp
        return dec[..., None, None] * carry + st, carry

    init = jnp.zeros_like(states[:, 0])
    _, prev = lax.scan(step, init, (jnp.moveaxis(states, 1, 0), jnp.moveaxis(chunk_decay, 1, 0)))
    prev = jnp.moveaxis(prev, 0, 1)
    y_off = jnp.einsum('bclgn,bcgrpn,bcgrl->bclgrp', Cc, prev, jnp.exp(a_cs))
    return (y_diag + y_off).reshape(b, s, h, p).astype(x.dtype)


def _ssd(z, xbc, dt_raw, conv_w, conv_b, dt_bias, A_log, D, norm_w):
    b, s, _ = z.shape
    xbc = jax.nn.silu(_causal_dwconv(xbc, conv_w, conv_b))
    xs, Bm, Cm = jnp.split(xbc, [SSD_WIDTH, SSD_WIDTH + SSD_GROUPS * SSD_STATE], axis=-1)
    xs = xs.reshape(b, s, SSD_HEADS, SSD_HEAD_DIM)
    Bm = Bm.reshape(b, s, SSD_GROUPS, SSD_STATE)
    Cm = Cm.reshape(b, s, SSD_GROUPS, SSD_STATE)
    dt = jax.nn.softplus((dt_raw + dt_bias).astype(jnp.float32))
    A = -jnp.exp(A_log.astype(jnp.float32))
    y = _ssd_scan(xs, dt, A, Bm, Cm, math.gcd(SSD_CHUNK, s))
    y = y + D[:, None] * xs
    y = y.reshape(b, s, SSD_WIDTH) * jax.nn.silu(z)
    yg = y.reshape(b, s, SSD_GROUPS, SSD_WIDTH // SSD_GROUPS).astype(jnp.float32)
    yg = yg * lax.rsqrt(jnp.mean(yg * yg, axis=-1, keepdims=True) + RMS_EPS)
    return (yg.reshape(b, s, SSD_WIDTH) * norm_w).astype(z.dtype)


def _conv_ffn(h, w_gate, w_up, conv_w, conv_b, w_down):
    gate = _causal_dwconv(h @ w_gate, conv_w, conv_b)
    return (jax.nn.silu(gate) * (h @ w_up)) @ w_down


def setup_inputs(seed: int = 0) -> dict:
    key = jax.random.key(seed)
    ks = jax.random.split(key, 32)
    L = DEPTH
    nrm = lambda k, shape, sc: sc * jax.random.normal(k, shape, jnp.float32)
    dt0 = jnp.exp(jax.random.uniform(ks[14], (L, SSD_HEADS), jnp.float32, math.log(1e-3), math.log(1e-1)))
    return {
        "x": jax.random.normal(ks[0], (BATCH, SEQ, D_MODEL), jnp.float32),
        "norm_mix_w": 1.0 + nrm(ks[1], (L, D_MODEL), 0.02),
        "w_in": nrm(ks[2], (L, D_MODEL, IN_COLS), D_MODEL ** -0.5),
        "cmp_pos_k": nrm(ks[3], (L, CMP_BLOCK, NSA_HEAD_DIM), 0.5),
        "cmp_w1_k": nrm(ks[4], (L, CMP_BLOCK * NSA_HEAD_DIM, CMP_HIDDEN), (CMP_BLOCK * NSA_HEAD_DIM) ** -0.5),
        "cmp_b1_k": nrm(ks[5], (L, CMP_HIDDEN), 0.02),
        "cmp_w2_k": nrm(ks[6], (L, CMP_HIDDEN, NSA_HEAD_DIM), CMP_HIDDEN ** -0.5),
        "cmp_pos_v": nrm(ks[7], (L, CMP_BLOCK, NSA_HEAD_DIM), 0.5),
        "cmp_w1_v": nrm(ks[8], (L, CMP_BLOCK * NSA_HEAD_DIM, CMP_HIDDEN), (CMP_BLOCK * NSA_HEAD_DIM) ** -0.5),
        "cmp_b1_v": nrm(ks[9], (L, CMP_HIDDEN), 0.02),
        "cmp_w2_v": nrm(ks[10], (L, CMP_HIDDEN, NSA_HEAD_DIM), CMP_HIDDEN ** -0.5),
        "nsa_norm_w": 1.0 + nrm(ks[11], (L, NSA_WIDTH), 0.02),
        "ssd_conv_w": nrm(ks[12], (L, SSD_CONV, SSD_XBC), SSD_CONV ** -0.5),
        "ssd_conv_b": nrm(ks[13], (L, SSD_XBC), 0.02),
        "ssd_dt_bias": dt0 + jnp.log(-jnp.expm1(-dt0)),
        "ssd_A_log": jnp.log(jax.random.uniform(ks[15], (L, SSD_HEADS), jnp.float32, 1.0, 16.0)),
        "ssd_D": 1.0 + nrm(ks[16], (L, SSD_HEADS), 0.1),
        "ssd_norm_w": 1.0 + nrm(ks[17], (L, SSD_WIDTH), 0.02),
        "w_out": nrm(ks[18], (L, D_MIX, D_MODEL), D_MIX ** -0.5),
        "norm_ffn_w": 1.0 + nrm(ks[19], (L, D_MODEL), 0.02),
        "w_gate": nrm(ks[20], (L, D_MODEL, D_FF), D_MODEL ** -0.5),
        "w_up": nrm(ks[21], (L, D_MODEL, D_FF), D_MODEL ** -0.5),
        "ffn_conv_w": nrm(ks[22], (L, FFN_CONV, D_FF), FFN_CONV ** -0.5),
        "ffn_conv_b": nrm(ks[23], (L, D_FF), 0.02),
        "w_down": nrm(ks[24], (L, D_FF, D_MODEL), D_FF ** -0.5),
        "norm_final_w": 1.0 + nrm(ks[25], (D_MODEL,), 0.02),
    }


def reference(x, norm_mix_w, w_in, cmp_pos_k, cmp_w1_k, cmp_b1_k, cmp_w2_k, cmp_pos_v, cmp_w1_v,
              cmp_b1_v, cmp_w2_v, nsa_norm_w, ssd_conv_w, ssd_conv_b, ssd_dt_bias, ssd_A_log, ssd_D,
              ssd_norm_w, w_out, norm_ffn_w, w_gate, w_up, ffn_conv_w, ffn_conv_b, w_down, norm_final_w):
    for i in range(DEPTH):
        h = _rmsnorm(x, norm_mix_w[i])
        q, kc, vc, ks_, vs_, kw, vw, gl, z, xbc, dtr = jnp.split(h @ w_in[i], IN_SPLITS, axis=-1)
        o_attn = _nsa(q, kc, vc, ks_, vs_, kw, vw, gl,
                      cmp_pos_k[i], cmp_w1_k[i], cmp_b1_k[i], cmp_w2_k[i],
                      cmp_pos_v[i], cmp_w1_v[i], cmp_b1_v[i], cmp_w2_v[i], nsa_norm_w[i])
        o_ssd = _ssd(z, xbc, dtr, ssd_conv_w[i], ssd_conv_b[i], ssd_dt_bias[i], ssd_A_log[i],
                     ssd_D[i], ssd_norm_w[i])
        x = x + jnp.concatenate([o_attn, o_ssd], axis=-1) @ w_out[i]
        h = _rmsnorm(x, norm_ffn_w[i])
        x = x + _conv_ffn(h, w_gate[i], w_up[i], ffn_conv_w[i], ffn_conv_b[i], w_down[i])
    return _rmsnorm(x, norm_final_w)
```

```python
import functools
import math

import numpy as np
import jax
import jax.numpy as jnp
from jax import lax
from jax.experimental import pallas as pl
from jax.experimental.pallas import tpu as pltpu

F32 = jnp.float32
BF16 = jnp.bfloat16

D_MODEL = 1024
NSA_HEADS = 8
NSA_KV_GROUPS = 2
NSA_HEAD_DIM = 64
NSA_REP = NSA_HEADS // NSA_KV_GROUPS
NSA_WIDTH = NSA_HEADS * NSA_HEAD_DIM
KV_WIDTH = NSA_KV_GROUPS * NSA_HEAD_DIM
CMP_BLOCK = 32
CMP_STRIDE = 16
CMP_HIDDEN = 128
SLC_BLOCK = 64
SLC_TOP = 16
WIN = 512
SSD_HEADS = 8
SSD_HEAD_DIM = 64
SSD_WIDTH = SSD_HEADS * SSD_HEAD_DIM
SSD_GROUPS = 2
SSD_STATE = 128
SSD_CONV = 4
SSD_CHUNK = 256
SSD_XBC = SSD_WIDTH + 2 * SSD_GROUPS * SSD_STATE
D_FF = 2816
FFN_CONV = 3
RMS_EPS = 1e-6
NEG = -1e30
FORCE_BONUS = 1e4

IN_SIZES = [NSA_WIDTH, KV_WIDTH, KV_WIDTH, KV_WIDTH, KV_WIDTH, KV_WIDTH, KV_WIDTH,
            3 * NSA_HEADS, SSD_WIDTH, SSD_XBC, SSD_HEADS]
IN_SPLITS = [int(v) for v in np.cumsum(IN_SIZES)[:-1]]

LANE = 128
HALO = 8
ATT_TQ = 256
ATT_TK = 256
FFN_TF = 256
VMEM_LIMIT = 56 * 1024 * 1024

_NT = (((1,), (1,)), ((), ()))
_TN = (((0,), (0,)), ((), ()))


def _mm(a, b):
    return jnp.dot(a, b, preferred_element_type=F32)


def _mm_nt(a, b):
    return lax.dot_general(a, b, _NT, preferred_element_type=F32)


def _rms(x, w):
    return x * lax.rsqrt(jnp.mean(x * x, axis=-1, keepdims=True) + RMS_EPS) * w


def _silu(x):
    return x * jax.nn.sigmoid(x)


def _params(*sem):
    return pltpu.CompilerParams(dimension_semantics=sem, vmem_limit_bytes=VMEM_LIMIT)


def _full(shape):
    n = len(shape)
    return pl.BlockSpec(shape, lambda *_: (0,) * n)


_PROJ_WIDTHS = (NSA_WIDTH, KV_WIDTH, KV_WIDTH, 2 * KV_WIDTH, 2 * KV_WIDTH, LANE,
                SSD_WIDTH, SSD_XBC, LANE)


def _pack_w_in(w):
    q, kc, vc, ks, vs, kw, vw, gl, z, xbc, dt = jnp.split(w, IN_SPLITS, axis=-1)
    pad = lambda a: jnp.pad(a, ((0, 0), (0, LANE - a.shape[1])))
    return jnp.concatenate([q, kc, vc, ks, vs, kw, vw, pad(gl), z, xbc, pad(dt)], axis=-1).astype(BF16)


def _in_proj_body(x_ref, nw_ref, w_ref, *out_refs):
    h = _rms(x_ref[...], nw_ref[...]).astype(BF16)
    off = 0
    for ref in out_refs:
        n = ref.shape[-1]
        ref[...] = _mm(h, w_ref[:, off:off + n])
        off += n


def _in_proj(x2, nw, w_packed, tm=512):
    t = x2.shape[0]
    ncol = w_packed.shape[1]
    return pl.pallas_call(
        _in_proj_body,
        grid=(t // tm,),
        in_specs=[pl.BlockSpec((tm, D_MODEL), lambda i: (i, 0)),
                  _full((1, D_MODEL)), _full((D_MODEL, ncol))],
        out_specs=[pl.BlockSpec((tm, n), lambda i: (i, 0)) for n in _PROJ_WIDTHS],
        out_shape=[jax.ShapeDtypeStruct((t, n), F32) for n in _PROJ_WIDTHS],
        compiler_params=_params("parallel"),
        name="in_proj",
    )(x2, nw, w_packed)


def _pack_compress(pos, w1, b1, w2):
    half = CMP_BLOCK // 2
    g = NSA_KV_GROUPS
    eye = jnp.eye(g, dtype=F32)
    w1r = w1.reshape(CMP_BLOCK, NSA_HEAD_DIM, CMP_HIDDEN)
    w1e = jnp.einsum('ldn,gh->lgdhn', w1r, eye).reshape(2, half * g * NSA_HEAD_DIM, g * CMP_HIDDEN)
    pose = jnp.broadcast_to(pos[:, None, :], (CMP_BLOCK, g, NSA_HEAD_DIM)).reshape(2, half * g * NSA_HEAD_DIM)
    b1e = jnp.tile(b1, g)[None, :]
    w2e = jnp.einsum('nd,gh->gnhd', w2, eye).reshape(g * CMP_HIDDEN, g * NSA_HEAD_DIM)
    return pose, w1e.astype(BF16), b1e, w2e.astype(BF16)


def _compress_body(kc_ref, vc_ref, pk_ref, wk1_ref, bk1_ref, wk2_ref,
                   pv_ref, wv1_ref, bv1_ref, wv2_ref, ko_ref, vo_ref):
    def one(r_ref, pos_ref, w1_ref, b1_ref, w2_ref, o_ref):
        r = r_ref[0]
        ncp = r.shape[0]
        lo = _mm((r + pos_ref[0:1, :]).astype(BF16), w1_ref[0])
        hi = _mm((r + pos_ref[1:2, :]).astype(BF16), w1_ref[1])
        hid = lo + pltpu.roll(hi, ncp - 1, 0) + b1_ref[...]
        hid = jax.nn.gelu(hid)
        o_ref[0] = _mm(hid.astype(BF16), w2_ref[...])

    one(kc_ref, pk_ref, wk1_ref, bk1_ref, wk2_ref, ko_ref)
    one(vc_ref, pv_ref, wv1_ref, bv1_ref, wv2_ref, vo_ref)


def _compress(kc3, vc3, pk, pv):
    b, ncp, wide = kc3.shape
    blk = pl.BlockSpec((1, ncp, wide), lambda i: (i, 0, 0))
    oblk = pl.BlockSpec((1, ncp, KV_WIDTH), lambda i: (i, 0, 0))
    wspecs = lambda p: [_full(a.shape) for a in p]
    return pl.pallas_call(
        _compress_body,
        grid=(b,),
        in_specs=[blk, blk] + wspecs(pk) + wspecs(pv),
        out_specs=[oblk, oblk],
        out_shape=[jax.ShapeDtypeStruct((b, ncp, KV_WIDTH), F32)] * 2,
        compiler_params=_params("parallel"),
        name="compress",
    )(kc3, vc3, *pk, *pv)


def _nsa_consts(s):
    nc_pad = s // CMP_STRIDE
    ns = s // SLC_BLOCK
    c0 = np.arange(nc_pad)[:, None] * CMP_STRIDE
    j0 = np.arange(LANE)[None, :] * SLC_BLOCK
    ovl = ((c0 < j0 + SLC_BLOCK) & (c0 + CMP_BLOCK > j0) & (np.arange(LANE)[None, :] < ns))
    ovl = np.tile(ovl.astype(np.float32), (NSA_REP, 1))
    key_blk = (np.arange(s) // SLC_BLOCK).reshape(s // ATT_TK, 1, ATT_TK)
    expand = (np.arange(LANE)[None, :, None] == key_blk).astype(np.float32)
    return jnp.asarray(ovl, BF16), jnp.asarray(expand, BF16)


def _nsa_body(q_ref, kcmp_ref, vcmp_ref, kvs_ref, kvw_ref, gl_ref, ovl_ref, exp_ref, nw_ref, o_ref,
              *, seq):
    tq, tk = ATT_TQ, ATT_TK
    hd, rep = NSA_HEAD_DIM, NSA_REP
    i = pl.program_id(1)
    t0 = i * tq
    ns = seq // SLC_BLOCK
    ntop = min(SLC_TOP, ns)
    ncp = kcmp_ref.shape[1]
    row_t = t0 + lax.broadcasted_iota(jnp.int32, (tq, 1), 0)
    gates = jax.nn.sigmoid(gl_ref[0])
    key_iota = lax.broadcasted_iota(jnp.int32, (tq, tk), 1)

    def flash(kv_ref, g, qs, lo, hi, mask_fn):
        def step(kt, carry):
            m, l, acc = carry
            k0 = pl.multiple_of(kt * tk, tk)
            kblk = kv_ref[0, pl.ds(k0, tk), g * hd:(g + 1) * hd].astype(BF16)
            vblk = kv_ref[0, pl.ds(k0, tk), KV_WIDTH + g * hd:KV_WIDTH + (g + 1) * hd].astype(BF16)
            s = _mm_nt(qs, kblk).reshape(rep, tq, tk)
            ok = mask_fn(kt, k0 + key_iota)
            s = jnp.where(ok[None], s, NEG)
            m_new = jnp.maximum(m, jnp.max(s, axis=-1, keepdims=True))
            alpha = jnp.exp(m - m_new)
            p = jnp.exp(s - m_new)
            l = alpha * l + jnp.sum(p, axis=-1, keepdims=True)
            pv = _mm(p.reshape(rep * tq, tk).astype(BF16), vblk).reshape(rep, tq, hd)
            return m_new, l, alpha * acc + pv

        init = (jnp.full((rep, tq, 1), NEG, F32), jnp.zeros((rep, tq, 1), F32),
                jnp.zeros((rep, tq, hd), F32))
        _, l, acc = lax.fori_loop(lo, hi, step, init)
        return acc / l

    outs = []
    for g in range(NSA_KV_GROUPS):
        qg = [(q_ref[0, :, (g * rep + r) * hd:(g * rep + r + 1) * hd] * (hd ** -0.5)).astype(BF16)
              for r in range(rep)]
        kc = kcmp_ref[0, :, g * hd:(g + 1) * hd].astype(BF16)
        vc = vcmp_ref[0, :, g * hd:(g + 1) * hd].astype(BF16)
        c_iota = lax.broadcasted_iota(jnp.int32, (tq, ncp), 1)
        cmask = (c_iota * CMP_STRIDE + (CMP_BLOCK - 1)) <= row_t
        ps, o_cmp = [], []
        for r in range(rep):
            s = jnp.where(cmask, _mm_nt(qg[r], kc), NEG)
            e = jnp.exp(s - jnp.max(s, axis=-1, keepdims=True))
            p = jnp.where(cmask, e / jnp.sum(e, axis=-1, keepdims=True), 0.0).astype(BF16)
            ps.append(p)
            o_cmp.append(_mm(p, vc))
        imp = _mm(jnp.concatenate(ps, axis=1), ovl_ref[...])
        j_iota = lax.broadcasted_iota(jnp.int32, (tq, LANE), 1)
        cur = jnp.right_shift(row_t, int(math.log2(SLC_BLOCK)))
        forced = (j_iota == 0) | (j_iota == cur) | (j_iota == cur - 1)
        score = jnp.where(j_iota <= cur, imp + jnp.where(forced, FORCE_BONUS, 0.0), NEG)
        rank = jnp.zeros((tq, LANE), F32)
        for k in range(ns):
            sk = score[:, k:k + 1]
            beats = (sk > score) | ((sk == score) & (j_iota > k))
            rank = rank + jnp.where(beats, 1.0, 0.0)
        sel = jnp.where(rank < ntop, 1.0, 0.0).astype(BF16)

        qs = jnp.concatenate(qg, axis=0)

        def slc_mask(kt, key_pos):
            chosen = _mm(sel, exp_ref[kt])
            return (chosen > 0.5) & (key_pos <= row_t)

        def win_mask(kt, key_pos):
            rel = row_t - key_pos
            return (rel >= 0) & (rel < WIN)

        o_slc = flash(kvs_ref, g, qs, 0, i + 1, slc_mask)
        o_win = flash(kvw_ref, g, qs, jnp.maximum(i - WIN // tk, 0), i + 1, win_mask)
        for r in range(rep):
            c = (g * rep + r) * 3
            outs.append(gates[:, c:c + 1] * o_cmp[r] + gates[:, c + 1:c + 2] * o_slc[r]
                        + gates[:, c + 2:c + 3] * o_win[r])
    o_ref[0] = _rms(jnp.concatenate(outs, axis=1), nw_ref[...])


def _nsa(q3, kcmp, vcmp, kvs3, kvw3, gl3, ovl, expand, nw):
    b, s, _ = q3.shape
    tq = ATT_TQ
    ncp = kcmp.shape[1]
    row = lambda w: pl.BlockSpec((1, tq, w), lambda bi, i: (bi, i, 0))
    per_b = lambda n, w: pl.BlockSpec((1, n, w), lambda bi, i: (bi, 0, 0))
    return pl.pallas_call(
        functools.partial(_nsa_body, seq=s),
        grid=(b, s // tq),
        in_specs=[row(NSA_WIDTH), per_b(ncp, KV_WIDTH), per_b(ncp, KV_WIDTH),
                  per_b(s, 2 * KV_WIDTH), per_b(s, 2 * KV_WIDTH), row(LANE),
                  _full(ovl.shape), _full(expand.shape), _full((1, NSA_WIDTH))],
        out_specs=row(NSA_WIDTH),
        out_shape=jax.ShapeDtypeStruct((b, s, NSA_WIDTH), F32),
        compiler_params=_params("parallel", "arbitrary"),
        name="nsa_attn",
    )(q3, kcmp, vcmp, kvs3, kvw3, gl3, ovl, expand, nw)


def _shifted_rows(x, halo, sh, row):
    r = pltpu.roll(x, sh, 0)
    for j in range(sh):
        r = jnp.where(row == j, halo[HALO - sh + j:HALO - sh + j + 1, :], r)
    return r


def _ssd_body(z_ref, xbc_ref, dt_ref, dtt_ref, cw_ref, cb_ref, dtb_row_ref, dtb_col_ref,
              alog_row_ref, alog_col_ref, d_ref, nw_ref, y_ref, state_ref, halo_ref):
    L = SSD_CHUNK
    hd, nst = SSD_HEAD_DIM, SSD_STATE
    rep = SSD_HEADS // SSD_GROUPS
    gw = rep * hd

    @pl.when(pl.program_id(1) == 0)
    def _():
        state_ref[...] = jnp.zeros_like(state_ref)
        halo_ref[...] = jnp.zeros_like(halo_ref)

    row = lax.broadcasted_iota(jnp.int32, (L, 1), 0)
    xr = xbc_ref[0]
    halo = halo_ref[...]
    conv = cb_ref[...] + cw_ref[SSD_CONV - 1:SSD_CONV, :] * xr
    for sh in range(1, SSD_CONV):
        conv = conv + cw_ref[SSD_CONV - 1 - sh:SSD_CONV - sh, :] * _shifted_rows(xr, halo, sh, row)
    halo_ref[...] = xr[L - HALO:L, :]
    xc = _silu(conv)
    xs = xc[:, :SSD_WIDTH]
    bm = xc[:, SSD_WIDTH:SSD_WIDTH + SSD_GROUPS * nst]
    cm = xc[:, SSD_WIDTH + SSD_GROUPS * nst:]

    dt_col = jax.nn.softplus(dt_ref[0] + dtb_row_ref[...])
    a_col = dt_col * (-jnp.exp(alog_row_ref[...]))
    i0 = lax.broadcasted_iota(jnp.int32, (L, L), 0)
    i1 = lax.broadcasted_iota(jnp.int32, (L, L), 1)
    tri = i0 >= i1
    acs_col = jnp.dot(jnp.where(tri, 1.0, 0.0), a_col, precision=lax.Precision.HIGHEST,
                      preferred_element_type=F32)
    dt_row = jax.nn.softplus(dtt_ref[0] + dtb_col_ref[...])
    a_row = dt_row * (-jnp.exp(alog_col_ref[...]))
    acs_row = jnp.dot(a_row, jnp.where(i0 <= i1, 1.0, 0.0), precision=lax.Precision.HIGHEST,
                      preferred_element_type=F32)
    acs_last = acs_col[L - 1:L, :]

    ys = []
    for g in range(SSD_GROUPS):
        bg = bm[:, g * nst:(g + 1) * nst].astype(BF16)
        cg = cm[:, g * nst:(g + 1) * nst].astype(BF16)
        cb = _mm_nt(cg, bg)
        state = state_ref[g]
        y_off = _mm(cg, state.astype(BF16))
        sx, cdec = [], []
        for r in range(rep):
            h = g * rep + r
            col = acs_col[:, h:h + 1]
            seg = col - acs_row[h:h + 1, :]
            lmat = jnp.where(tri, jnp.exp(jnp.where(tri, seg, 0.0)), 0.0)
            xh = xs[:, h * hd:(h + 1) * hd]
            xdt = xh * dt_col[:, h:h + 1]
            y = _mm((cb * lmat).astype(BF16), xdt.astype(BF16))
            y = y + jnp.exp(col) * y_off[:, r * hd:(r + 1) * hd] + d_ref[:, h * hd:(h + 1) * hd] * xh
            ys.append(y)
            last = acs_last[:, h:h + 1]
            sx.append(xdt * jnp.exp(last - col))
            cdec.append(jnp.broadcast_to(jnp.exp(last), (1, hd)))
        sxg = jnp.concatenate(sx, axis=1).astype(BF16)
        new = lax.dot_general(bg, sxg, _TN, preferred_element_type=F32)
        state_ref[g] = state * jnp.concatenate(cdec, axis=1) + new
    y = jnp.concatenate(ys, axis=1) * _silu(z_ref[0])
    outs = []
    for g in range(SSD_GROUPS):
        outs.append(_rms(y[:, g * gw:(g + 1) * gw], nw_ref[:, g * gw:(g + 1) * gw]))
    y_ref[0] = jnp.concatenate(outs, axis=1)


def _ssd(z3, xbc3, dt3, dtt3, cw, cb, dtb_row, dtb_col, alog_row, alog_col, d_exp, nw):
    b, s, _ = z3.shape
    L = SSD_CHUNK
    row = lambda w: pl.BlockSpec((1, L, w), lambda bi, c: (bi, c, 0))
    small = [cw, cb, dtb_row, dtb_col, alog_row, alog_col, d_exp, nw]
    return pl.pallas_call(
        _ssd_body,
        grid=(b, s // L),
        in_specs=[row(SSD_WIDTH), row(SSD_XBC), row(LANE),
                  pl.BlockSpec((1, SSD_HEADS, L), lambda bi, c: (bi, 0, c))]
                 + [_full(a.shape) for a in small],
        out_specs=row(SSD_WIDTH),
        out_shape=jax.ShapeDtypeStruct((b, s, SSD_WIDTH), F32),
        scratch_shapes=[pltpu.VMEM((SSD_GROUPS, SSD_STATE, SSD_WIDTH // SSD_GROUPS), F32),
                        pltpu.VMEM((HALO, SSD_XBC), F32)],
        compiler_params=_params("parallel", "arbitrary"),
        name="ssd",
    )(z3, xbc3, dt3, dtt3, *small)


def _out_proj_body(x_ref, a_ref, s_ref, wa_ref, ws_ref, o_ref):
    o_ref[...] = (x_ref[...] + _mm(a_ref[...].astype(BF16), wa_ref[...])
                  + _mm(s_ref[...].astype(BF16), ws_ref[...]))


def _out_proj(x2, a2, s2, wa, ws, tm=512):
    t = x2.shape[0]
    row = lambda w: pl.BlockSpec((tm, w), lambda i: (i, 0))
    return pl.pallas_call(
        _out_proj_body,
        grid=(t // tm,),
        in_specs=[row(D_MODEL), row(NSA_WIDTH), row(SSD_WIDTH), _full(wa.shape), _full(ws.shape)],
        out_specs=row(D_MODEL),
        out_shape=jax.ShapeDtypeStruct((t, D_MODEL), F32),
        input_output_aliases={0: 0},
        compiler_params=_params("parallel"),
        name="out_proj",
    )(x2, a2, s2, wa, ws)


def _ffn_body(x_ref, nw_ref, wg_ref, wu_ref, cw_ref, cb_ref, wd_ref, o_ref, halo_ref):
    tm = x_ref.shape[1]
    nf = wg_ref.shape[0]

    @pl.when(pl.program_id(1) == 0)
    def _():
        halo_ref[...] = jnp.zeros_like(halo_ref)

    x = x_ref[0]
    h = _rms(x, nw_ref[...]).astype(BF16)
    row = lax.broadcasted_iota(jnp.int32, (tm, 1), 0)
    acc = jnp.zeros((tm, D_MODEL), F32)
    for f in range(nf):
        gp = _mm(h, wg_ref[f])
        up = _mm(h, wu_ref[f])
        halo = halo_ref[f]
        cw = cw_ref[f]
        gate = cb_ref[f] + cw[FFN_CONV - 1:FFN_CONV, :] * gp
        for sh in range(1, FFN_CONV):
            gate = gate + cw[FFN_CONV - 1 - sh:FFN_CONV - sh, :] * _shifted_rows(gp, halo, sh, row)
        halo_ref[f] = gp[tm - HALO:tm, :]
        acc = acc + _mm((_silu(gate) * up).astype(BF16), wd_ref[f])
    o_ref[0] = x + acc


def _ffn(x3, nw, wg, wu, cw, cb, wd, tm=512):
    b, s, _ = x3.shape
    nf = wg.shape[0]
    row = pl.BlockSpec((1, tm, D_MODEL), lambda bi, i: (bi, i, 0))
    return pl.pallas_call(
        _ffn_body,
        grid=(b, s // tm),
        in_specs=[row, _full(nw.shape), _full(wg.shape), _full(wu.shape), _full(cw.shape),
                  _full(cb.shape), _full(wd.shape)],
        out_specs=row,
        out_shape=jax.ShapeDtypeStruct((b, s, D_MODEL), F32),
        scratch_shapes=[pltpu.VMEM((nf, HALO, FFN_TF), F32)],
        input_output_aliases={0: 0},
        compiler_params=_params("parallel", "arbitrary"),
        name="conv_ffn",
    )(x3, nw, wg, wu, cw, cb, wd)


def _final_norm_body(x_ref, w_ref, o_ref):
    o_ref[...] = _rms(x_ref[...], w_ref[...])


def _final_norm(x2, w, tm=1024):
    t = x2.shape[0]
    row = pl.BlockSpec((tm, D_MODEL), lambda i: (i, 0))
    return pl.pallas_call(
        _final_norm_body,
        grid=(t // tm,),
        in_specs=[row, _full(w.shape)],
        out_specs=row,
        out_shape=jax.ShapeDtypeStruct((t, D_MODEL), F32),
        compiler_params=_params("parallel"),
        name="final_norm",
    )(x2, w)


def _pad_lanes(v):
    return jnp.pad(v, (0, LANE - v.shape[0]))[None, :]


def kernel(x, norm_mix_w, w_in, cmp_pos_k, cmp_w1_k, cmp_b1_k, cmp_w2_k, cmp_pos_v, cmp_w1_v,
           cmp_b1_v, cmp_w2_v, nsa_norm_w, ssd_conv_w, ssd_conv_b, ssd_dt_bias, ssd_A_log, ssd_D,
           ssd_norm_w, w_out, norm_ffn_w, w_gate, w_up, ffn_conv_w, ffn_conv_b, w_down, norm_final_w):
    b, s, d = x.shape
    depth = w_in.shape[0]
    t = b * s
    nf = D_FF // FFN_TF
    ovl, expand = _nsa_consts(s)
    x2 = x.reshape(t, d)
    for i in range(depth):
        q, kc, vc, kvs, kvw, gl, z, xbc, dt = _in_proj(x2, norm_mix_w[i][None, :], _pack_w_in(w_in[i]))
        ncp = s // CMP_STRIDE
        kcmp, vcmp = _compress(
            kc.reshape(b, ncp, CMP_STRIDE * KV_WIDTH), vc.reshape(b, ncp, CMP_STRIDE * KV_WIDTH),
            _pack_compress(cmp_pos_k[i], cmp_w1_k[i], cmp_b1_k[i], cmp_w2_k[i]),
            _pack_compress(cmp_pos_v[i], cmp_w1_v[i], cmp_b1_v[i], cmp_w2_v[i]))
        o_attn = _nsa(q.reshape(b, s, -1), kcmp, vcmp, kvs.reshape(b, s, -1), kvw.reshape(b, s, -1),
                      gl.reshape(b, s, -1), ovl, expand, nsa_norm_w[i][None, :])
        dt3 = dt.reshape(b, s, LANE)
        dtt3 = jnp.swapaxes(dt3[:, :, :SSD_HEADS], 1, 2)
        o_ssd = _ssd(z.reshape(b, s, -1), xbc.reshape(b, s, -1), dt3, dtt3,
                     ssd_conv_w[i], ssd_conv_b[i][None, :],
                     _pad_lanes(ssd_dt_bias[i]), ssd_dt_bias[i][:, None],
                     _pad_lanes(ssd_A_log[i]), ssd_A_log[i][:, None],
                     jnp.repeat(ssd_D[i], SSD_HEAD_DIM)[None, :], ssd_norm_w[i][None, :])
        wo = w_out[i].astype(BF16)
        x2 = _out_proj(x2, o_attn.reshape(t, -1), o_ssd.reshape(t, -1), wo[:NSA_WIDTH], wo[NSA_WIDTH:])
        x3 = _ffn(x2.reshape(b, s, d), norm_ffn_w[i][None, :],
                  w_gate[i].reshape(d, nf, FFN_TF).transpose(1, 0, 2).astype(BF16),
                  w_up[i].reshape(d, nf, FFN_TF).transpose(1, 0, 2).astype(BF16),
                  ffn_conv_w[i].reshape(FFN_CONV, nf, FFN_TF).transpose(1, 0, 2),
                  ffn_conv_b[i].reshape(nf, 1, FFN_TF),
                  w_down[i].reshape(nf, FFN_TF, d).astype(BF16))
        x2 = x3.reshape(t, d)
    return _final_norm(x2, norm_final_w[None, :]).reshape(b, s, d)
```

```python
import functools
import math

import numpy as np
import jax
import jax.numpy as jnp
from jax import lax
from jax.experimental import pallas as pl
from jax.experimental.pallas import tpu as pltpu

F32 = jnp.float32
BF16 = jnp.bfloat16

D_MODEL = 1024
NSA_HEADS = 8
NSA_KV_GROUPS = 2
NSA_HEAD_DIM = 64
NSA_REP = NSA_HEADS // NSA_KV_GROUPS
NSA_WIDTH = NSA_HEADS * NSA_HEAD_DIM
KV_WIDTH = NSA_KV_GROUPS * NSA_HEAD_DIM
CMP_BLOCK = 32
CMP_STRIDE = 16
CMP_HIDDEN = 128
SLC_BLOCK = 64
SLC_TOP = 16
WIN = 512
SSD_HEADS = 8
SSD_HEAD_DIM = 64
SSD_WIDTH = SSD_HEADS * SSD_HEAD_DIM
SSD_GROUPS = 2
SSD_STATE = 128
SSD_CONV = 4
SSD_CHUNK = 256
SSD_XBC = SSD_WIDTH + 2 * SSD_GROUPS * SSD_STATE
D_FF = 2816
FFN_CONV = 3
RMS_EPS = 1e-6
NEG = -1e30
FORCE_BONUS = 1e4

IN_SIZES = [NSA_WIDTH, KV_WIDTH, KV_WIDTH, KV_WIDTH, KV_WIDTH, KV_WIDTH, KV_WIDTH,
            3 * NSA_HEADS, SSD_WIDTH, SSD_XBC, SSD_HEADS]
IN_SPLITS = [int(v) for v in np.cumsum(IN_SIZES)[:-1]]

LANE = 128
SUBLANE = 8
HALO = SUBLANE
ATT_TQ = 256
ATT_TK = 256
FFN_TF = 256
VMEM_LIMIT = 56 * 1024 * 1024

_NT = (((1,), (1,)), ((), ()))
_TN = (((0,), (0,)), ((), ()))


def _mm(a, b):
    return jnp.dot(a, b, preferred_element_type=F32)


def _mm_nt(a, b):
    return lax.dot_general(a, b, _NT, preferred_element_type=F32)


def _rms(x, w):
    return x * lax.rsqrt(jnp.mean(x * x, axis=-1, keepdims=True) + RMS_EPS) * w


def _silu(x):
    return x * jax.nn.sigmoid(x)


def _params(*sem):
    return pltpu.CompilerParams(dimension_semantics=sem, vmem_limit_bytes=VMEM_LIMIT)


def _full(shape):
    n = len(shape)
    return pl.BlockSpec(shape, lambda *_: (0,) * n)


_PROJ_WIDTHS = (NSA_WIDTH, KV_WIDTH, KV_WIDTH, 2 * KV_WIDTH, 2 * KV_WIDTH, LANE,
                SSD_WIDTH, SSD_XBC, LANE)


def _pack_w_in(w):
    q, kc, vc, ks, vs, kw, vw, gl, z, xbc, dt = jnp.split(w, IN_SPLITS, axis=-1)
    pad = lambda a: jnp.pad(a, ((0, 0), (0, LANE - a.shape[1])))
    return jnp.concatenate([q, kc, vc, ks, vs, kw, vw, pad(gl), z, xbc, pad(dt)], axis=-1).astype(BF16)


def _in_proj_body(x_ref, nw_ref, w_ref, *out_refs):
    h = _rms(x_ref[...], nw_ref[...]).astype(BF16)
    off = 0
    for ref in out_refs:
        n = ref.shape[-1]
        ref[...] = _mm(h, w_ref[:, off:off + n])
        off += n


def _in_proj(x2, nw, w_packed, tm=512):
    t = x2.shape[0]
    ncol = w_packed.shape[1]
    return pl.pallas_call(
        _in_proj_body,
        grid=(t // tm,),
        in_specs=[pl.BlockSpec((tm, D_MODEL), lambda i: (i, 0)),
                  _full((1, D_MODEL)), _full((D_MODEL, ncol))],
        out_specs=[pl.BlockSpec((tm, n), lambda i: (i, 0)) for n in _PROJ_WIDTHS],
        out_shape=[jax.ShapeDtypeStruct((t, n), F32) for n in _PROJ_WIDTHS],
        compiler_params=_params("parallel"),
        name="in_proj",
    )(x2, nw, w_packed)


def _pack_compress(pos, w1, b1, w2):
    half = CMP_BLOCK // 2
    g = NSA_KV_GROUPS
    eye = jnp.eye(g, dtype=F32)
    w1r = w1.reshape(CMP_BLOCK, NSA_HEAD_DIM, CMP_HIDDEN)
    w1e = jnp.einsum('ldn,gh->lgdhn', w1r, eye).reshape(2, half * g * NSA_HEAD_DIM, g * CMP_HIDDEN)
    pose = jnp.broadcast_to(pos[:, None, :], (CMP_BLOCK, g, NSA_HEAD_DIM)).reshape(2, half * g * NSA_HEAD_DIM)
    b1e = jnp.tile(b1, g)[None, :]
    w2e = jnp.einsum('nd,gh->gnhd', w2, eye).reshape(g * CMP_HIDDEN, g * NSA_HEAD_DIM)
    return pose, w1e.astype(BF16), b1e, w2e.astype(BF16)


def _compress_body(kc_ref, vc_ref, pk_ref, wk1_ref, bk1_ref, wk2_ref,
                   pv_ref, wv1_ref, bv1_ref, wv2_ref, ko_ref, vo_ref):
    def one(r_ref, pos_ref, w1_ref, b1_ref, w2_ref, o_ref):
        r = r_ref[0]
        ncp = r.shape[0]
        lo = _mm((r + pos_ref[0:1, :]).astype(BF16), w1_ref[0])
        hi = _mm((r + pos_ref[1:2, :]).astype(BF16), w1_ref[1])
        hid = lo + pltpu.roll(hi, ncp - 1, 0) + b1_ref[...]
        hid = jax.nn.gelu(hid)
        o_ref[0] = _mm(hid.astype(BF16), w2_ref[...])

    one(kc_ref, pk_ref, wk1_ref, bk1_ref, wk2_ref, ko_ref)
    one(vc_ref, pv_ref, wv1_ref, bv1_ref, wv2_ref, vo_ref)


def _compress(kc3, vc3, pk, pv):
    b, ncp, wide = kc3.shape
    blk = pl.BlockSpec((1, ncp, wide), lambda i: (i, 0, 0))
    oblk = pl.BlockSpec((1, ncp, KV_WIDTH), lambda i: (i, 0, 0))
    wspecs = lambda p: [_full(a.shape) for a in p]
    return pl.pallas_call(
        _compress_body,
        grid=(b,),
        in_specs=[blk, blk] + wspecs(pk) + wspecs(pv),
        out_specs=[oblk, oblk],
        out_shape=[jax.ShapeDtypeStruct((b, ncp, KV_WIDTH), F32)] * 2,
        compiler_params=_params("parallel"),
        name="compress",
    )(kc3, vc3, *pk, *pv)


SEL_ROWS = 32
V_ROWS = 80
assert ATT_TQ == ATT_TK and WIN == 2 * ATT_TK and KV_WIDTH == LANE and 2 * NSA_HEAD_DIM == LANE


def _nsa_consts(s):
    ncp = s // CMP_STRIDE
    ns = s // SLC_BLOCK
    assert ns <= SEL_ROWS
    c0 = np.arange(ncp)[None, :] * CMP_STRIDE
    jj = np.arange(SEL_ROWS)[:, None]
    ovl_t = (c0 < jj * SLC_BLOCK + SLC_BLOCK) & (c0 + CMP_BLOCK > jj * SLC_BLOCK) & (jj < ns)
    onehot = np.zeros((s, LANE), np.float32)
    onehot[np.arange(s), NSA_HEAD_DIM + np.arange(s) // SLC_BLOCK] = 1.0
    k = np.arange(ATT_TK)[:, None]
    t = np.arange(ATT_TQ)[None, :]
    causal = np.where(k <= t, 0.0, NEG).astype(np.float32)
    winlo = np.where(k > t, 0.0, NEG).astype(np.float32)
    return (jnp.asarray(ovl_t.astype(np.float32), BF16), jnp.asarray(onehot),
            jnp.asarray(causal), jnp.asarray(winlo))


def _nsa_body(q_ref, kcmp_ref, vcmp_ref, kvs_ref, kvw_ref, gl_ref, ovl_ref, onehot_ref, causal_ref,
              winlo_ref, nw_ref, o_ref, kaug_ref, vaug_ref, acc_ref, m_ref, *, seq):
    tq, tk, hd, rep = ATT_TQ, ATT_TK, NSA_HEAD_DIM, NSA_REP
    i = pl.program_id(1)
    t0 = i * tq
    ntop = min(SLC_TOP, seq // SLC_BLOCK)
    ncp = kcmp_ref.shape[1]
    nq = rep * tq

    @pl.when(i == 0)
    def _():
        lane = lax.broadcasted_iota(jnp.int32, (tk, LANE), 1)
        ones_col = jnp.where(lane == hd, 1.0, 0.0)
        for br, kv_ref in enumerate((kvs_ref, kvw_ref)):
            def chunk(c, carry, br=br, kv_ref=kv_ref):
                r0 = pl.multiple_of(c * tk, tk)
                k128 = kv_ref[0, pl.ds(r0, tk), 0:KV_WIDTH]
                v128 = kv_ref[0, pl.ds(r0, tk), KV_WIDTH:2 * KV_WIDTH]
                extra = onehot_ref[pl.ds(r0, tk), :] if br == 0 else 0.0
                for g in range(NSA_KV_GROUPS):
                    kk = k128 if g == 0 else pltpu.roll(k128, hd, 1)
                    vv = v128 if g == 0 else pltpu.roll(v128, hd, 1)
                    kaug_ref[2 * br + g, pl.ds(r0, tk), :] = jnp.where(lane < hd, kk, extra).astype(BF16)
                    vt = jnp.where(lane < hd, vv, ones_col).T
                    vaug_ref[2 * br + g, :, pl.ds(r0, tk)] = vt[0:V_ROWS].astype(BF16)
                return carry
            lax.fori_loop(0, seq // tk, chunk, 0)

    def tile(slot, kt, q_all, bias_ref):
        k0 = pl.multiple_of(kt * tk, tk)
        s_t = _mm_nt(kaug_ref[slot, pl.ds(k0, tk), :], q_all)
        if bias_ref is not None:
            s_t = s_t + jnp.concatenate([bias_ref[...]] * rep, axis=1)
        m_old = m_ref[...]
        m_new = jnp.maximum(m_old, jnp.max(s_t, axis=0, keepdims=True))
        p = jnp.exp(s_t - m_new).astype(BF16)
        acc_ref[...] = (jnp.exp(m_old - m_new) * acc_ref[...]
                        + _mm(vaug_ref[slot, :, pl.ds(k0, tk)], p))
        m_ref[...] = m_new

    def start():
        m_ref[...] = jnp.full(m_ref.shape, NEG, F32)
        acc_ref[...] = jnp.zeros(acc_ref.shape, F32)

    def finish():
        acc = acc_ref[...]
        return acc[0:hd] / acc[hd:hd + 1]

    lane_q = lax.broadcasted_iota(jnp.int32, (tq, LANE), 1)
    lane_c = lax.broadcasted_iota(jnp.int32, (ncp, LANE), 1)
    qv = q_ref[0]
    gates_t = jax.nn.sigmoid(gl_ref[0]).T
    vc_t = vcmp_ref[0].T
    t_lane = t0 + (lax.broadcasted_iota(jnp.int32, (ncp, nq), 1) & (tq - 1))
    cmask = (lax.broadcasted_iota(jnp.int32, (ncp, nq), 0) * CMP_STRIDE + (CMP_BLOCK - 1)) <= t_lane
    j_sub = lax.broadcasted_iota(jnp.int32, (SEL_ROWS, tq), 0)
    cur = jnp.right_shift(t0 + lax.broadcasted_iota(jnp.int32, (SEL_ROWS, tq), 1),
                          int(math.log2(SLC_BLOCK)))
    bonus = jnp.where((j_sub == 0) | (j_sub == cur) | (j_sub == cur - 1), FORCE_BONUS, 0.0)
    valid = j_sub <= cur
    j8 = lax.broadcasted_iota(jnp.int32, (SUBLANE, tq), 0)
    ngrp = SEL_ROWS // SUBLANE

    outs_t = []
    for g in range(NSA_KV_GROUPS):
        heads = []
        for r in range(rep):
            hidx = g * rep + r
            c128 = qv[:, (hidx // 2) * LANE:(hidx // 2 + 1) * LANE]
            if hidx % 2:
                c128 = pltpu.roll(c128, hd, 1)
            heads.append(jnp.where(lane_q < hd, c128 * (hd ** -0.5), 0.0))
        q_plain = jnp.concatenate(heads, axis=0).astype(BF16)

        kc = kcmp_ref[0]
        if g:
            kc = pltpu.roll(kc, hd, 1)
        kc = jnp.where(lane_c < hd, kc, 0.0).astype(BF16)
        s_t = jnp.where(cmask, _mm_nt(kc, q_plain), NEG)
        e = jnp.exp(s_t - jnp.max(s_t, axis=0, keepdims=True))
        p = jnp.where(cmask, e / jnp.sum(e, axis=0, keepdims=True), 0.0).astype(BF16)
        o_cmp = _mm(vc_t[g * hd:(g + 1) * hd].astype(BF16), p)
        imp4 = _mm(ovl_ref[...], p)
        imp = imp4[:, 0:tq]
        for r in range(1, rep):
            imp = imp + imp4[:, r * tq:(r + 1) * tq]

        score = jnp.where(valid, imp + bonus, NEG)
        grp = [score[SUBLANE * a:SUBLANE * (a + 1)] for a in range(ngrp)]
        rank = [jnp.zeros((SUBLANE, tq), F32) for _ in range(ngrp)]
        for k in range(SEL_ROWS):
            sk = jnp.broadcast_to(score[k:k + 1, :], (SUBLANE, tq))
            for a in range(ngrp):
                if k < SUBLANE * a:
                    inc = jnp.where(sk >= grp[a], 1.0, 0.0)
                elif k >= SUBLANE * (a + 1):
                    inc = jnp.where(sk > grp[a], 1.0, 0.0)
                else:
                    inc = jnp.where(j8 > k - SUBLANE * a, jnp.where(sk >= grp[a], 1.0, 0.0),
                                    jnp.where(sk > grp[a], 1.0, 0.0))
                rank[a] = rank[a] + inc
        bias_rows = [jnp.where(rk < ntop, 0.0, NEG) for rk in rank]
        b128 = jnp.concatenate([jnp.zeros((hd, tq), F32)] + bias_rows
                               + [jnp.zeros((LANE - hd - SEL_ROWS, tq), F32)], axis=0)
        bq = b128.T
        q_slc = jnp.concatenate([h + bq for h in heads], axis=0).astype(BF16)

        def slc_step(kt, carry, g=g, q_slc=q_slc):
            tile(g, kt, q_slc, None)
            return carry

        start()
        lax.fori_loop(0, i, slc_step, 0)
        tile(g, i, q_slc, causal_ref)
        o_slc = finish()

        start()

        @pl.when(i >= 2)
        def _(g=g, q_plain=q_plain):
            tile(2 + g, i - 2, q_plain, winlo_ref)

        @pl.when(i >= 1)
        def _(g=g, q_plain=q_plain):
            tile(2 + g, i - 1, q_plain, None)

        tile(2 + g, i, q_plain, causal_ref)
        o_win = finish()

        for r in range(rep):
            c = (g * rep + r) * 3
            sl = slice(r * tq, (r + 1) * tq)
            outs_t.append(gates_t[c:c + 1] * o_cmp[:, sl] + gates_t[c + 1:c + 2] * o_slc[:, sl]
                          + gates_t[c + 2:c + 3] * o_win[:, sl])
    o_ref[0] = _rms(jnp.concatenate(outs_t, axis=0).T, nw_ref[...])


def _nsa(q3, kcmp, vcmp, kvs3, kvw3, gl3, consts, nw):
    b, s, _ = q3.shape
    tq = ATT_TQ
    ncp = kcmp.shape[1]
    row = lambda w: pl.BlockSpec((1, tq, w), lambda bi, i: (bi, i, 0))
    per_b = lambda n, w: pl.BlockSpec((1, n, w), lambda bi, i: (bi, 0, 0))
    return pl.pallas_call(
        functools.partial(_nsa_body, seq=s),
        grid=(b, s // tq),
        in_specs=[row(NSA_WIDTH), per_b(ncp, KV_WIDTH), per_b(ncp, KV_WIDTH),
                  per_b(s, 2 * KV_WIDTH), per_b(s, 2 * KV_WIDTH), row(LANE)]
                 + [_full(c.shape) for c in consts] + [_full((1, NSA_WIDTH))],
        out_specs=row(NSA_WIDTH),
        out_shape=jax.ShapeDtypeStruct((b, s, NSA_WIDTH), F32),
        scratch_shapes=[pltpu.VMEM((2 * NSA_KV_GROUPS, s, LANE), BF16),
                        pltpu.VMEM((2 * NSA_KV_GROUPS, V_ROWS, s), BF16),
                        pltpu.VMEM((V_ROWS, NSA_REP * tq), F32),
                        pltpu.VMEM((1, NSA_REP * tq), F32)],
        compiler_params=_params("parallel", "arbitrary"),
        name="nsa_attn",
    )(q3, kcmp, vcmp, kvs3, kvw3, gl3, *consts, nw)


def _shifted_rows(x, halo, sh, row):
    r = pltpu.roll(x, sh, 0)
    for j in range(sh):
        r = jnp.where(row == j, halo[HALO - sh + j:HALO - sh + j + 1, :], r)
    return r


def _ssd_body(z_ref, xbc_ref, dt_ref, dtt_ref, cw_ref, cb_ref, dtb_row_ref, dtb_col_ref,
              alog_row_ref, alog_col_ref, d_ref, nw_ref, y_ref, state_ref, halo_ref):
    L = SSD_CHUNK
    hd, nst = SSD_HEAD_DIM, SSD_STATE
    rep = SSD_HEADS // SSD_GROUPS
    gw = rep * hd

    @pl.when(pl.program_id(1) == 0)
    def _():
        state_ref[...] = jnp.zeros_like(state_ref)
        halo_ref[...] = jnp.zeros_like(halo_ref)

    row = lax.broadcasted_iota(jnp.int32, (L, 1), 0)
    xr = xbc_ref[0]
    halo = halo_ref[...]
    conv = cb_ref[...] + cw_ref[SSD_CONV - 1:SSD_CONV, :] * xr
    for sh in range(1, SSD_CONV):
        conv = conv + cw_ref[SSD_CONV - 1 - sh:SSD_CONV - sh, :] * _shifted_rows(xr, halo, sh, row)
    halo_ref[...] = xr[L - HALO:L, :]
    xc = _silu(conv)
    xs = xc[:, :SSD_WIDTH]
    bm = xc[:, SSD_WIDTH:SSD_WIDTH + SSD_GROUPS * nst]
    cm = xc[:, SSD_WIDTH + SSD_GROUPS * nst:]

    dt_col = jax.nn.softplus(dt_ref[0] + dtb_row_ref[...])
    a_col = dt_col * (-jnp.exp(alog_row_ref[...]))
    i0 = lax.broadcasted_iota(jnp.int32, (L, L), 0)
    i1 = lax.broadcasted_iota(jnp.int32, (L, L), 1)
    tri = i0 >= i1
    acs_col = jnp.dot(jnp.where(tri, 1.0, 0.0), a_col, precision=lax.Precision.HIGHEST,
                      preferred_element_type=F32)
    dt_row = jax.nn.softplus(dtt_ref[0] + dtb_col_ref[...])
    a_row = dt_row * (-jnp.exp(alog_col_ref[...]))
    acs_row = jnp.dot(a_row, jnp.where(i0 <= i1, 1.0, 0.0), precision=lax.Precision.HIGHEST,
                      preferred_element_type=F32)
    acs_last = acs_col[L - 1:L, :]

    ys = []
    for g in range(SSD_GROUPS):
        bg = bm[:, g * nst:(g + 1) * nst].astype(BF16)
        cg = cm[:, g * nst:(g + 1) * nst].astype(BF16)
        cb = _mm_nt(cg, bg)
        state = state_ref[g]
        y_off = _mm(cg, state.astype(BF16))
        sx, cdec = [], []
        for r in range(rep):
            h = g * rep + r
            col = acs_col[:, h:h + 1]
            seg = col - acs_row[h:h + 1, :]
            lmat = jnp.where(tri, jnp.exp(jnp.where(tri, seg, 0.0)), 0.0)
            xh = xs[:, h * hd:(h + 1) * hd]
            xdt = xh * dt_col[:, h:h + 1]
            y = _mm((cb * lmat).astype(BF16), xdt.astype(BF16))
            y = y + jnp.exp(col) * y_off[:, r * hd:(r + 1) * hd] + d_ref[:, h * hd:(h + 1) * hd] * xh
            ys.append(y)
            last = acs_last[:, h:h + 1]
            sx.append(xdt * jnp.exp(last - col))
            cdec.append(jnp.broadcast_to(jnp.exp(last), (1, hd)))
        sxg = jnp.concatenate(sx, axis=1).astype(BF16)
        new = lax.dot_general(bg, sxg, _TN, preferred_element_type=F32)
        state_ref[g] = state * jnp.concatenate(cdec, axis=1) + new
    y = jnp.concatenate(ys, axis=1) * _silu(z_ref[0])
    outs = []
    for g in range(SSD_GROUPS):
        outs.append(_rms(y[:, g * gw:(g + 1) * gw], nw_ref[:, g * gw:(g + 1) * gw]))
    y_ref[0] = jnp.concatenate(outs, axis=1)


def _ssd(z3, xbc3, dt3, dtt3, cw, cb, dtb_row, dtb_col, alog_row, alog_col, d_exp, nw):
    b, s, _ = z3.shape
    L = SSD_CHUNK
    row = lambda w: pl.BlockSpec((1, L, w), lambda bi, c: (bi, c, 0))
    small = [cw, cb, dtb_row, dtb_col, alog_row, alog_col, d_exp, nw]
    return pl.pallas_call(
        _ssd_body,
        grid=(b, s // L),
        in_specs=[row(SSD_WIDTH), row(SSD_XBC), row(LANE),
                  pl.BlockSpec((1, SSD_HEADS, L), lambda bi, c: (bi, 0, c))]
                 + [_full(a.shape) for a in small],
        out_specs=row(SSD_WIDTH),
        out_shape=jax.ShapeDtypeStruct((b, s, SSD_WIDTH), F32),
        scratch_shapes=[pltpu.VMEM((SSD_GROUPS, SSD_STATE, SSD_WIDTH // SSD_GROUPS), F32),
                        pltpu.VMEM((HALO, SSD_XBC), F32)],
        compiler_params=_params("parallel", "arbitrary"),
        name="ssd",
    )(z3, xbc3, dt3, dtt3, *small)


def _out_proj_body(x_ref, a_ref, s_ref, wa_ref, ws_ref, o_ref):
    o_ref[...] = (x_ref[...] + _mm(a_ref[...].astype(BF16), wa_ref[...])
                  + _mm(s_ref[...].astype(BF16), ws_ref[...]))


def _out_proj(x2, a2, s2, wa, ws, tm=512):
    t = x2.shape[0]
    row = lambda w: pl.BlockSpec((tm, w), lambda i: (i, 0))
    return pl.pallas_call(
        _out_proj_body,
        grid=(t // tm,),
        in_specs=[row(D_MODEL), row(NSA_WIDTH), row(SSD_WIDTH), _full(wa.shape), _full(ws.shape)],
        out_specs=row(D_MODEL),
        out_shape=jax.ShapeDtypeStruct((t, D_MODEL), F32),
        input_output_aliases={0: 0},
        compiler_params=_params("parallel"),
        name="out_proj",
    )(x2, a2, s2, wa, ws)


def _ffn_body(x_ref, nw_ref, wg_ref, wu_ref, cw_ref, cb_ref, wd_ref, o_ref, halo_ref):
    tm = x_ref.shape[1]
    nf = wg_ref.shape[0]

    @pl.when(pl.program_id(1) == 0)
    def _():
        halo_ref[...] = jnp.zeros_like(halo_ref)

    x = x_ref[0]
    h = _rms(x, nw_ref[...]).astype(BF16)
    row = lax.broadcasted_iota(jnp.int32, (tm, 1), 0)
    acc = jnp.zeros((tm, D_MODEL), F32)
    for f in range(nf):
        gp = _mm(h, wg_ref[f])
        up = _mm(h, wu_ref[f])
        halo = halo_ref[f]
        cw = cw_ref[f]
        gate = cb_ref[f] + cw[FFN_CONV - 1:FFN_CONV, :] * gp
        for sh in range(1, FFN_CONV):
            gate = gate + cw[FFN_CONV - 1 - sh:FFN_CONV - sh, :] * _shifted_rows(gp, halo, sh, row)
        halo_ref[f] = gp[tm - HALO:tm, :]
        acc = acc + _mm((_silu(gate) * up).astype(BF16), wd_ref[f])
    o_ref[0] = x + acc


def _ffn(x3, nw, wg, wu, cw, cb, wd, tm=512):
    b, s, _ = x3.shape
    nf = wg.shape[0]
    row = pl.BlockSpec((1, tm, D_MODEL), lambda bi, i: (bi, i, 0))
    return pl.pallas_call(
        _ffn_body,
        grid=(b, s // tm),
        in_specs=[row, _full(nw.shape), _full(wg.shape), _full(wu.shape), _full(cw.shape),
                  _full(cb.shape), _full(wd.shape)],
        out_specs=row,
        out_shape=jax.ShapeDtypeStruct((b, s, D_MODEL), F32),
        scratch_shapes=[pltpu.VMEM((nf, HALO, FFN_TF), F32)],
        input_output_aliases={0: 0},
        compiler_params=_params("parallel", "arbitrary"),
        name="conv_ffn",
    )(x3, nw, wg, wu, cw, cb, wd)


def _final_norm_body(x_ref, w_ref, o_ref):
    o_ref[...] = _rms(x_ref[...], w_ref[...])


def _final_norm(x2, w, tm=1024):
    t = x2.shape[0]
    row = pl.BlockSpec((tm, D_MODEL), lambda i: (i, 0))
    return pl.pallas_call(
        _final_norm_body,
        grid=(t // tm,),
        in_specs=[row, _full(w.shape)],
        out_specs=row,
        out_shape=jax.ShapeDtypeStruct((t, D_MODEL), F32),
        compiler_params=_params("parallel"),
        name="final_norm",
    )(x2, w)


def _pad_lanes(v):
    return jnp.pad(v, (0, LANE - v.shape[0]))[None, :]


def kernel(x, norm_mix_w, w_in, cmp_pos_k, cmp_w1_k, cmp_b1_k, cmp_w2_k, cmp_pos_v, cmp_w1_v,
           cmp_b1_v, cmp_w2_v, nsa_norm_w, ssd_conv_w, ssd_conv_b, ssd_dt_bias, ssd_A_log, ssd_D,
           ssd_norm_w, w_out, norm_ffn_w, w_gate, w_up, ffn_conv_w, ffn_conv_b, w_down, norm_final_w):
    b, s, d = x.shape
    depth = w_in.shape[0]
    t = b * s
    nf = D_FF // FFN_TF
    nsa_consts = _nsa_consts(s)
    x2 = x.reshape(t, d)
    for i in range(depth):
        q, kc, vc, kvs, kvw, gl, z, xbc, dt = _in_proj(x2, norm_mix_w[i][None, :], _pack_w_in(w_in[i]))
        ncp = s // CMP_STRIDE
        kcmp, vcmp = _compress(
            kc.reshape(b, ncp, CMP_STRIDE * KV_WIDTH), vc.reshape(b, ncp, CMP_STRIDE * KV_WIDTH),
            _pack_compress(cmp_pos_k[i], cmp_w1_k[i], cmp_b1_k[i], cmp_w2_k[i]),
            _pack_compress(cmp_pos_v[i], cmp_w1_v[i], cmp_b1_v[i], cmp_w2_v[i]))
        o_attn = _nsa(q.reshape(b, s, -1), kcmp, vcmp, kvs.reshape(b, s, -1), kvw.reshape(b, s, -1),
                      gl.reshape(b, s, -1), nsa_consts, nsa_norm_w[i][None, :])
        dt3 = dt.reshape(b, s, LANE)
        dtt3 = jnp.swapaxes(dt3[:, :, :SSD_HEADS], 1, 2)
        o_ssd = _ssd(z.reshape(b, s, -1), xbc.reshape(b, s, -1), dt3, dtt3,
                     ssd_conv_w[i], ssd_conv_b[i][None, :],
                     _pad_lanes(ssd_dt_bias[i]), ssd_dt_bias[i][:, None],
                     _pad_lanes(ssd_A_log[i]), ssd_A_log[i][:, None],
                     jnp.repeat(ssd_D[i], SSD_HEAD_DIM)[None, :], ssd_norm_w[i][None, :])
        wo = w_out[i].astype(BF16)
        x2 = _out_proj(x2, o_attn.reshape(t, -1), o_ssd.reshape(t, -1), wo[:NSA_WIDTH], wo[NSA_WIDTH:])
        x3 = _ffn(x2.reshape(b, s, d), norm_ffn_w[i][None, :],
                  w_gate[i].reshape(d, nf, FFN_TF).transpose(1, 0, 2).astype(BF16),
                  w_up[i].reshape(d, nf, FFN_TF).transpose(1, 0, 2).astype(BF16),
                  ffn_conv_w[i].reshape(FFN_CONV, nf, FFN_TF).transpose(1, 0, 2),
                  ffn_conv_b[i].reshape(nf, 1, FFN_TF),
                  w_down[i].reshape(nf, FFN_TF, d).astype(BF16))
        x2 = x3.reshape(t, d)
    return _final_norm(x2, norm_final_w[None, :]).reshape(b, s, d)
```

```python
import functools
import math

import numpy as np
import jax
import jax.numpy as jnp
from jax import lax
from jax.experimental import pallas as pl
from jax.experimental.pallas import tpu as pltpu

F32 = jnp.float32
BF16 = jnp.bfloat16

D_MODEL = 1024
NSA_HEADS = 8
NSA_KV_GROUPS = 2
NSA_HEAD_DIM = 64
NSA_REP = NSA_HEADS // NSA_KV_GROUPS
NSA_WIDTH = NSA_HEADS * NSA_HEAD_DIM
KV_WIDTH = NSA_KV_GROUPS * NSA_HEAD_DIM
CMP_BLOCK = 32
CMP_STRIDE = 16
CMP_HIDDEN = 128
SLC_BLOCK = 64
SLC_TOP = 16
WIN = 512
SSD_HEADS = 8
SSD_HEAD_DIM = 64
SSD_WIDTH = SSD_HEADS * SSD_HEAD_DIM
SSD_GROUPS = 2
SSD_STATE = 128
SSD_CONV = 4
SSD_CHUNK = 256
SSD_XBC = SSD_WIDTH + 2 * SSD_GROUPS * SSD_STATE
D_FF = 2816
FFN_CONV = 3
RMS_EPS = 1e-6
NEG = -1e30
FORCE_BONUS = 1e4

IN_SIZES = [NSA_WIDTH, KV_WIDTH, KV_WIDTH, KV_WIDTH, KV_WIDTH, KV_WIDTH, KV_WIDTH,
            3 * NSA_HEADS, SSD_WIDTH, SSD_XBC, SSD_HEADS]
IN_SPLITS = [int(v) for v in np.cumsum(IN_SIZES)[:-1]]

LANE = 128
SUBLANE = 8
HALO = SUBLANE
ATT_TQ = 256
ATT_TK = 256
FFN_TF = 256
FFN_TM = 1024
VMEM_LIMIT = 56 * 1024 * 1024

_NT = (((1,), (1,)), ((), ()))
_TN = (((0,), (0,)), ((), ()))


def _mm(a, b):
    return jnp.dot(a, b, preferred_element_type=F32)


def _mm_nt(a, b):
    return lax.dot_general(a, b, _NT, preferred_element_type=F32)


def _rms(x, w):
    return x * lax.rsqrt(jnp.mean(x * x, axis=-1, keepdims=True) + RMS_EPS) * w


def _silu(x):
    return x * jax.nn.sigmoid(x)


def _params(*sem):
    return pltpu.CompilerParams(dimension_semantics=sem, vmem_limit_bytes=VMEM_LIMIT)


def _full(shape):
    n = len(shape)
    return pl.BlockSpec(shape, lambda *_: (0,) * n)


_PROJ_WIDTHS = (NSA_WIDTH, KV_WIDTH, KV_WIDTH, 2 * KV_WIDTH, 2 * KV_WIDTH, LANE,
                SSD_WIDTH, SSD_XBC, LANE)


def _pack_w_in(w):
    q, kc, vc, ks, vs, kw, vw, gl, z, xbc, dt = jnp.split(w, IN_SPLITS, axis=-1)
    pad = lambda a: jnp.pad(a, ((0, 0), (0, LANE - a.shape[1])))
    return jnp.concatenate([q, kc, vc, ks, vs, kw, vw, pad(gl), z, xbc, pad(dt)], axis=-1).astype(BF16)


def _in_proj_body(x_ref, nw_ref, w_ref, *out_refs):
    h = _rms(x_ref[...], nw_ref[...]).astype(BF16)
    off = 0
    for ref in out_refs:
        n = ref.shape[-1]
        ref[...] = _mm(h, w_ref[:, off:off + n])
        off += n


def _in_proj(x2, nw, w_packed, tm=512):
    t = x2.shape[0]
    ncol = w_packed.shape[1]
    return pl.pallas_call(
        _in_proj_body,
        grid=(t // tm,),
        in_specs=[pl.BlockSpec((tm, D_MODEL), lambda i: (i, 0)),
                  _full((1, D_MODEL)), _full((D_MODEL, ncol))],
        out_specs=[pl.BlockSpec((tm, n), lambda i: (i, 0)) for n in _PROJ_WIDTHS],
        out_shape=[jax.ShapeDtypeStruct((t, n), F32) for n in _PROJ_WIDTHS],
        compiler_params=_params("parallel"),
        name="in_proj",
    )(x2, nw, w_packed)


def _pack_compress(pos, w1, b1, w2):
    half = CMP_BLOCK // 2
    g = NSA_KV_GROUPS
    eye = jnp.eye(g, dtype=F32)
    w1r = w1.reshape(CMP_BLOCK, NSA_HEAD_DIM, CMP_HIDDEN)
    w1e = jnp.einsum('ldn,gh->lgdhn', w1r, eye).reshape(2, half * g * NSA_HEAD_DIM, g * CMP_HIDDEN)
    pose = jnp.broadcast_to(pos[:, None, :], (CMP_BLOCK, g, NSA_HEAD_DIM)).reshape(2, half * g * NSA_HEAD_DIM)
    b1e = jnp.tile(b1, g)[None, :]
    w2e = jnp.einsum('nd,gh->gnhd', w2, eye).reshape(g * CMP_HIDDEN, g * NSA_HEAD_DIM)
    return pose, w1e.astype(BF16), b1e, w2e.astype(BF16)


def _compress_body(kc_ref, vc_ref, pk_ref, wk1_ref, bk1_ref, wk2_ref,
                   pv_ref, wv1_ref, bv1_ref, wv2_ref, ko_ref, vo_ref):
    def one(r_ref, pos_ref, w1_ref, b1_ref, w2_ref, o_ref):
        r = r_ref[0]
        ncp = r.shape[0]
        lo = _mm((r + pos_ref[0:1, :]).astype(BF16), w1_ref[0])
        hi = _mm((r + pos_ref[1:2, :]).astype(BF16), w1_ref[1])
        hid = lo + pltpu.roll(hi, ncp - 1, 0) + b1_ref[...]
        hid = jax.nn.gelu(hid)
        o_ref[0] = _mm(hid.astype(BF16), w2_ref[...])

    one(kc_ref, pk_ref, wk1_ref, bk1_ref, wk2_ref, ko_ref)
    one(vc_ref, pv_ref, wv1_ref, bv1_ref, wv2_ref, vo_ref)


def _compress(kc3, vc3, pk, pv):
    b, ncp, wide = kc3.shape
    blk = pl.BlockSpec((1, ncp, wide), lambda i: (i, 0, 0))
    oblk = pl.BlockSpec((1, ncp, KV_WIDTH), lambda i: (i, 0, 0))
    wspecs = lambda p: [_full(a.shape) for a in p]
    return pl.pallas_call(
        _compress_body,
        grid=(b,),
        in_specs=[blk, blk] + wspecs(pk) + wspecs(pv),
        out_specs=[oblk, oblk],
        out_shape=[jax.ShapeDtypeStruct((b, ncp, KV_WIDTH), F32)] * 2,
        compiler_params=_params("parallel"),
        name="compress",
    )(kc3, vc3, *pk, *pv)


BIAS_ZERO, BIAS_CAUSAL, BIAS_WINLO, BIAS_NONE = range(4)
SEL_ROWS = 32
V_ROWS = 80
assert ATT_TQ == ATT_TK and WIN == 2 * ATT_TK and KV_WIDTH == LANE and 2 * NSA_HEAD_DIM == LANE


def _nsa_consts(s):
    ncp = s // CMP_STRIDE
    ns = s // SLC_BLOCK
    assert ns <= SEL_ROWS
    c0 = np.arange(ncp)[None, :] * CMP_STRIDE
    jj = np.arange(SEL_ROWS)[:, None]
    ovl_t = (c0 < jj * SLC_BLOCK + SLC_BLOCK) & (c0 + CMP_BLOCK > jj * SLC_BLOCK) & (jj < ns)
    onehot = np.zeros((s, LANE), np.float32)
    onehot[np.arange(s), NSA_HEAD_DIM + np.arange(s) // SLC_BLOCK] = 1.0
    k = np.arange(ATT_TK)[:, None]
    t = np.arange(ATT_TQ)[None, :]
    bias = np.stack([np.zeros((ATT_TK, ATT_TQ)),
                     np.where(k <= t, 0.0, NEG),
                     np.where(k > t, 0.0, NEG),
                     np.full((ATT_TK, ATT_TQ), NEG)]).astype(np.float32)
    return jnp.asarray(ovl_t.astype(np.float32), BF16), jnp.asarray(onehot), jnp.asarray(bias)


def _nsa_body(q_ref, kcmp_ref, vcmp_ref, kvs_ref, kvw_ref, gl_ref, ovl_ref, onehot_ref, bias_ref,
              nw_ref, o_ref, kaug_ref, vaug_ref, acc_ref, m_ref, *, seq):
    tq, tk, hd, rep = ATT_TQ, ATT_TK, NSA_HEAD_DIM, NSA_REP
    i = pl.program_id(1)
    t0 = i * tq
    ntop = min(SLC_TOP, seq // SLC_BLOCK)
    ncp = kcmp_ref.shape[1]
    nq = rep * tq

    @pl.when(i == 0)
    def _():
        lane = lax.broadcasted_iota(jnp.int32, (tk, LANE), 1)
        ones_col = jnp.where(lane == hd, 1.0, 0.0)
        for br, kv_ref in enumerate((kvs_ref, kvw_ref)):
            def chunk(c, carry, br=br, kv_ref=kv_ref):
                r0 = pl.multiple_of(c * tk, tk)
                k128 = kv_ref[0, pl.ds(r0, tk), 0:KV_WIDTH]
                v128 = kv_ref[0, pl.ds(r0, tk), KV_WIDTH:2 * KV_WIDTH]
                extra = onehot_ref[pl.ds(r0, tk), :] if br == 0 else 0.0
                for g in range(NSA_KV_GROUPS):
                    kk = k128 if g == 0 else pltpu.roll(k128, hd, 1)
                    vv = v128 if g == 0 else pltpu.roll(v128, hd, 1)
                    kaug_ref[2 * br + g, pl.ds(r0, tk), :] = jnp.where(lane < hd, kk, extra).astype(BF16)
                    vt = jnp.where(lane < hd, vv, ones_col).T
                    vaug_ref[2 * br + g, :, pl.ds(r0, tk)] = vt[0:V_ROWS].astype(BF16)
                return carry
            lax.fori_loop(0, seq // tk, chunk, 0)

    def scores(slot, kt, q_all):
        k0 = pl.multiple_of(kt * tk, tk)
        return _mm_nt(kaug_ref[slot, pl.ds(k0, tk), :], q_all)

    def absorb(items):
        stats = []
        for slot, kt, s_t in items:
            m_old = m_ref[slot]
            m_new = jnp.maximum(m_old, jnp.max(s_t, axis=0, keepdims=True))
            stats.append((m_old, m_new, jnp.exp(s_t - m_new).astype(BF16)))
        pvs = [_mm(vaug_ref[slot, :, pl.ds(pl.multiple_of(kt * tk, tk), tk)], p)
               for (slot, kt, _), (_, _, p) in zip(items, stats)]
        for (slot, _, _), (m_old, m_new, _), pv in zip(items, stats, pvs):
            acc_ref[slot] = jnp.exp(m_old - m_new) * acc_ref[slot] + pv
            m_ref[slot] = m_new

    def finish(slot):
        acc = acc_ref[slot]
        return acc[0:hd] / acc[hd:hd + 1]

    lane_q = lax.broadcasted_iota(jnp.int32, (tq, LANE), 1)
    lane_c = lax.broadcasted_iota(jnp.int32, (ncp, LANE), 1)
    qv = q_ref[0]
    gates_t = jax.nn.sigmoid(gl_ref[0]).T
    vc_t = vcmp_ref[0].T
    t_lane = t0 + (lax.broadcasted_iota(jnp.int32, (ncp, nq), 1) & (tq - 1))
    cmask = (lax.broadcasted_iota(jnp.int32, (ncp, nq), 0) * CMP_STRIDE + (CMP_BLOCK - 1)) <= t_lane
    j_sub = lax.broadcasted_iota(jnp.int32, (SEL_ROWS, tq), 0)
    cur = jnp.right_shift(t0 + lax.broadcasted_iota(jnp.int32, (SEL_ROWS, tq), 1),
                          int(math.log2(SLC_BLOCK)))
    bonus = jnp.where((j_sub == 0) | (j_sub == cur) | (j_sub == cur - 1), FORCE_BONUS, 0.0)
    valid = j_sub <= cur
    j8 = lax.broadcasted_iota(jnp.int32, (SUBLANE, tq), 0)
    ngrp = SEL_ROWS // SUBLANE

    q_plain, q_slc, o_cmp = [], [], []
    for g in range(NSA_KV_GROUPS):
        heads = []
        for r in range(rep):
            hidx = g * rep + r
            c128 = qv[:, (hidx // 2) * LANE:(hidx // 2 + 1) * LANE]
            if hidx % 2:
                c128 = pltpu.roll(c128, hd, 1)
            heads.append(jnp.where(lane_q < hd, c128 * (hd ** -0.5), 0.0))
        q_plain.append(jnp.concatenate(heads, axis=0).astype(BF16))

        kc = kcmp_ref[0]
        if g:
            kc = pltpu.roll(kc, hd, 1)
        kc = jnp.where(lane_c < hd, kc, 0.0).astype(BF16)
        s_t = jnp.where(cmask, _mm_nt(kc, q_plain[g]), NEG)
        e = jnp.exp(s_t - jnp.max(s_t, axis=0, keepdims=True))
        p = jnp.where(cmask, e / jnp.sum(e, axis=0, keepdims=True), 0.0).astype(BF16)
        o_cmp.append(_mm(vc_t[g * hd:(g + 1) * hd].astype(BF16), p))
        imp4 = _mm(ovl_ref[...], p)
        imp = imp4[:, 0:tq]
        for r in range(1, rep):
            imp = imp + imp4[:, r * tq:(r + 1) * tq]

        score = jnp.where(valid, imp + bonus, NEG)
        grp = [score[SUBLANE * a:SUBLANE * (a + 1)] for a in range(ngrp)]
        rank = [jnp.zeros((SUBLANE, tq), F32) for _ in range(ngrp)]
        for k in range(SEL_ROWS):
            sk = jnp.broadcast_to(score[k:k + 1, :], (SUBLANE, tq))
            for a in range(ngrp):
                if k < SUBLANE * a:
                    inc = jnp.where(sk >= grp[a], 1.0, 0.0)
                elif k >= SUBLANE * (a + 1):
                    inc = jnp.where(sk > grp[a], 1.0, 0.0)
                else:
                    inc = jnp.where(j8 > k - SUBLANE * a, jnp.where(sk >= grp[a], 1.0, 0.0),
                                    jnp.where(sk > grp[a], 1.0, 0.0))
                rank[a] = rank[a] + inc
        bias_rows = [jnp.where(rk < ntop, 0.0, NEG) for rk in rank]
        b128 = jnp.concatenate([jnp.zeros((hd, tq), F32)] + bias_rows
                               + [jnp.zeros((LANE - hd - SEL_ROWS, tq), F32)], axis=0)
        bq = b128.T
        q_slc.append(jnp.concatenate([h + bq for h in heads], axis=0).astype(BF16))

    m_ref[...] = jnp.full(m_ref.shape, NEG, F32)
    acc_ref[...] = jnp.zeros(acc_ref.shape, F32)
    groups = range(NSA_KV_GROUPS)
    tiled = lambda kind: jnp.concatenate([bias_ref[kind]] * rep, axis=1)

    def run(tiles):
        items = [(slot, kt, scores(slot, kt, q) if bias is None else scores(slot, kt, q) + bias)
                 for slot, kt, q, bias in tiles]
        for n in range(0, len(items), NSA_KV_GROUPS):
            absorb(items[n:n + NSA_KV_GROUPS])

    def slc_step(j, carry):
        run([(g, 2 * j + u, q_slc[g], None) for u in range(2) for g in groups])
        return carry

    lax.fori_loop(0, i // 2, slc_step, 0)
    odd = (i % 2) == 1
    tail = [(g, jnp.maximum(i - 1, 0), q_slc[g], tiled(jnp.where(odd, BIAS_ZERO, BIAS_NONE))) for g in groups]
    tail += [(g, i, q_slc[g], tiled(BIAS_CAUSAL)) for g in groups]
    for step, kind in enumerate((BIAS_WINLO, BIAS_ZERO, BIAS_CAUSAL)):
        kt = i - 2 + step
        bias = tiled(jnp.where(kt >= 0, kind, BIAS_NONE)) if step < 2 else tiled(kind)
        tail += [(2 + g, jnp.maximum(kt, 0), q_plain[g], bias) for g in groups]
    run(tail)

    outs_t = []
    for g in range(NSA_KV_GROUPS):
        o_slc = finish(g)
        o_win = finish(2 + g)
        for r in range(rep):
            c = (g * rep + r) * 3
            sl = slice(r * tq, (r + 1) * tq)
            outs_t.append(gates_t[c:c + 1] * o_cmp[g][:, sl] + gates_t[c + 1:c + 2] * o_slc[:, sl]
                          + gates_t[c + 2:c + 3] * o_win[:, sl])
    o_ref[0] = _rms(jnp.concatenate(outs_t, axis=0).T, nw_ref[...])


def _nsa(q3, kcmp, vcmp, kvs3, kvw3, gl3, consts, nw):
    b, s, _ = q3.shape
    tq = ATT_TQ
    ncp = kcmp.shape[1]
    row = lambda w: pl.BlockSpec((1, tq, w), lambda bi, i: (bi, i, 0))
    per_b = lambda n, w: pl.BlockSpec((1, n, w), lambda bi, i: (bi, 0, 0))
    return pl.pallas_call(
        functools.partial(_nsa_body, seq=s),
        grid=(b, s // tq),
        in_specs=[row(NSA_WIDTH), per_b(ncp, KV_WIDTH), per_b(ncp, KV_WIDTH),
                  per_b(s, 2 * KV_WIDTH), per_b(s, 2 * KV_WIDTH), row(LANE)]
                 + [_full(c.shape) for c in consts] + [_full((1, NSA_WIDTH))],
        out_specs=row(NSA_WIDTH),
        out_shape=jax.ShapeDtypeStruct((b, s, NSA_WIDTH), F32),
        scratch_shapes=[pltpu.VMEM((2 * NSA_KV_GROUPS, s, LANE), BF16),
                        pltpu.VMEM((2 * NSA_KV_GROUPS, V_ROWS, s), BF16),
                        pltpu.VMEM((2 * NSA_KV_GROUPS, V_ROWS, NSA_REP * tq), F32),
                        pltpu.VMEM((2 * NSA_KV_GROUPS, 1, NSA_REP * tq), F32)],
        compiler_params=_params("parallel", "arbitrary"),
        name="nsa_attn",
    )(q3, kcmp, vcmp, kvs3, kvw3, gl3, *consts, nw)


def _shifted_rows(x, halo, sh, row):
    r = pltpu.roll(x, sh, 0)
    for j in range(sh):
        r = jnp.where(row == j, halo[HALO - sh + j:HALO - sh + j + 1, :], r)
    return r


def _ssd_body(z_ref, xbc_ref, dt_ref, dtt_ref, cw_ref, cb_ref, dtb_row_ref, dtb_col_ref,
              alog_row_ref, alog_col_ref, d_ref, nw_ref, y_ref, state_ref, halo_ref):
    L = SSD_CHUNK
    hd, nst = SSD_HEAD_DIM, SSD_STATE
    rep = SSD_HEADS // SSD_GROUPS
    gw = rep * hd

    @pl.when(pl.program_id(1) == 0)
    def _():
        state_ref[...] = jnp.zeros_like(state_ref)
        halo_ref[...] = jnp.zeros_like(halo_ref)

    row = lax.broadcasted_iota(jnp.int32, (L, 1), 0)
    xr = xbc_ref[0]
    halo = halo_ref[...]
    conv = cb_ref[...] + cw_ref[SSD_CONV - 1:SSD_CONV, :] * xr
    for sh in range(1, SSD_CONV):
        conv = conv + cw_ref[SSD_CONV - 1 - sh:SSD_CONV - sh, :] * _shifted_rows(xr, halo, sh, row)
    halo_ref[...] = xr[L - HALO:L, :]
    xc = _silu(conv)
    xs = xc[:, :SSD_WIDTH]
    bm = xc[:, SSD_WIDTH:SSD_WIDTH + SSD_GROUPS * nst]
    cm = xc[:, SSD_WIDTH + SSD_GROUPS * nst:]

    dt_col = jax.nn.softplus(dt_ref[0] + dtb_row_ref[...])
    a_col = dt_col * (-jnp.exp(alog_row_ref[...]))
    i0 = lax.broadcasted_iota(jnp.int32, (L, L), 0)
    i1 = lax.broadcasted_iota(jnp.int32, (L, L), 1)
    tri = i0 >= i1
    acs_col = jnp.dot(jnp.where(tri, 1.0, 0.0), a_col, precision=lax.Precision.HIGHEST,
                      preferred_element_type=F32)
    dt_row = jax.nn.softplus(dtt_ref[0] + dtb_col_ref[...])
    a_row = dt_row * (-jnp.exp(alog_col_ref[...]))
    acs_row = jnp.dot(a_row, jnp.where(i0 <= i1, 1.0, 0.0), precision=lax.Precision.HIGHEST,
                      preferred_element_type=F32)
    acs_last = acs_col[L - 1:L, :]

    ys = []
    for g in range(SSD_GROUPS):
        bg = bm[:, g * nst:(g + 1) * nst].astype(BF16)
        cg = cm[:, g * nst:(g + 1) * nst].astype(BF16)
        cb = _mm_nt(cg, bg)
        state = state_ref[g]
        y_off = _mm(cg, state.astype(BF16))
        sx, cdec = [], []
        for r in range(rep):
            h = g * rep + r
            col = acs_col[:, h:h + 1]
            seg = col - acs_row[h:h + 1, :]
            lmat = jnp.where(tri, jnp.exp(jnp.where(tri, seg, 0.0)), 0.0)
            xh = xs[:, h * hd:(h + 1) * hd]
            xdt = xh * dt_col[:, h:h + 1]
            y = _mm((cb * lmat).astype(BF16), xdt.astype(BF16))
            y = y + jnp.exp(col) * y_off[:, r * hd:(r + 1) * hd] + d_ref[:, h * hd:(h + 1) * hd] * xh
            ys.append(y)
            last = acs_last[:, h:h + 1]
            sx.append(xdt * jnp.exp(last - col))
            cdec.append(jnp.broadcast_to(jnp.exp(last), (1, hd)))
        sxg = jnp.concatenate(sx, axis=1).astype(BF16)
        new = lax.dot_general(bg, sxg, _TN, preferred_element_type=F32)
        state_ref[g] = state * jnp.concatenate(cdec, axis=1) + new
    y = jnp.concatenate(ys, axis=1) * _silu(z_ref[0])
    outs = []
    for g in range(SSD_GROUPS):
        outs.append(_rms(y[:, g * gw:(g + 1) * gw], nw_ref[:, g * gw:(g + 1) * gw]))
    y_ref[0] = jnp.concatenate(outs, axis=1)


def _ssd(z3, xbc3, dt3, dtt3, cw, cb, dtb_row, dtb_col, alog_row, alog_col, d_exp, nw):
    b, s, _ = z3.shape
    L = SSD_CHUNK
    row = lambda w: pl.BlockSpec((1, L, w), lambda bi, c: (bi, c, 0))
    small = [cw, cb, dtb_row, dtb_col, alog_row, alog_col, d_exp, nw]
    return pl.pallas_call(
        _ssd_body,
        grid=(b, s // L),
        in_specs=[row(SSD_WIDTH), row(SSD_XBC), row(LANE),
                  pl.BlockSpec((1, SSD_HEADS, L), lambda bi, c: (bi, 0, c))]
                 + [_full(a.shape) for a in small],
        out_specs=row(SSD_WIDTH),
        out_shape=jax.ShapeDtypeStruct((b, s, SSD_WIDTH), F32),
        scratch_shapes=[pltpu.VMEM((SSD_GROUPS, SSD_STATE, SSD_WIDTH // SSD_GROUPS), F32),
                        pltpu.VMEM((HALO, SSD_XBC), F32)],
        compiler_params=_params("parallel", "arbitrary"),
        name="ssd",
    )(z3, xbc3, dt3, dtt3, *small)


def _out_proj_body(x_ref, a_ref, s_ref, wa_ref, ws_ref, o_ref):
    o_ref[...] = (x_ref[...] + _mm(a_ref[...].astype(BF16), wa_ref[...])
                  + _mm(s_ref[...].astype(BF16), ws_ref[...]))


def _out_proj(x2, a2, s2, wa, ws, tm=512):
    t = x2.shape[0]
    row = lambda w: pl.BlockSpec((tm, w), lambda i: (i, 0))
    return pl.pallas_call(
        _out_proj_body,
        grid=(t // tm,),
        in_specs=[row(D_MODEL), row(NSA_WIDTH), row(SSD_WIDTH), _full(wa.shape), _full(ws.shape)],
        out_specs=row(D_MODEL),
        out_shape=jax.ShapeDtypeStruct((t, D_MODEL), F32),
        compiler_params=_params("parallel"),
        name="out_proj",
    )(x2, a2, s2, wa, ws)


def _ffn_body(x_ref, nw_ref, wg_ref, wu_ref, cw_ref, cb_ref, wd_ref, o_ref, halo_ref, act_ref):
    tm = x_ref.shape[1]
    nf = wg_ref.shape[0]

    @pl.when(pl.program_id(1) == 0)
    def _():
        halo_ref[...] = jnp.zeros_like(halo_ref)

    x = x_ref[0]
    h = _rms(x, nw_ref[...]).astype(BF16)
    row = lax.broadcasted_iota(jnp.int32, (tm, 1), 0)
    for f in range(nf):
        gp = _mm(h, wg_ref[f])
        up = _mm(h, wu_ref[f])
        halo = halo_ref[f]
        cw = cw_ref[f]
        gate = cb_ref[f] + cw[FFN_CONV - 1:FFN_CONV, :] * gp
        for sh in range(1, FFN_CONV):
            gate = gate + cw[FFN_CONV - 1 - sh:FFN_CONV - sh, :] * _shifted_rows(gp, halo, sh, row)
        halo_ref[f] = gp[tm - HALO:tm, :]
        act_ref[:, f * FFN_TF:(f + 1) * FFN_TF] = (_silu(gate) * up).astype(BF16)
    o_ref[0] = x + _mm(act_ref[...], wd_ref[...])


def _ffn(x3, nw, wg, wu, cw, cb, wd, tm=FFN_TM):
    b, s, _ = x3.shape
    nf = wg.shape[0]
    row = pl.BlockSpec((1, tm, D_MODEL), lambda bi, i: (bi, i, 0))
    once = lambda a: pl.BlockSpec(a.shape, lambda *_: (0,) * a.ndim, pipeline_mode=pl.Buffered(1))
    return pl.pallas_call(
        _ffn_body,
        grid=(b, s // tm),
        in_specs=[row, _full(nw.shape), once(wg), once(wu), _full(cw.shape), _full(cb.shape), once(wd)],
        out_specs=row,
        out_shape=jax.ShapeDtypeStruct((b, s, D_MODEL), F32),
        scratch_shapes=[pltpu.VMEM((nf, HALO, FFN_TF), F32), pltpu.VMEM((tm, D_FF), BF16)],
        input_output_aliases={0: 0},
        compiler_params=_params("parallel", "arbitrary"),
        name="conv_ffn",
    )(x3, nw, wg, wu, cw, cb, wd)


def _final_norm_body(x_ref, w_ref, o_ref):
    o_ref[...] = _rms(x_ref[...], w_ref[...])


def _final_norm(x2, w, tm=1024):
    t = x2.shape[0]
    row = pl.BlockSpec((tm, D_MODEL), lambda i: (i, 0))
    return pl.pallas_call(
        _final_norm_body,
        grid=(t // tm,),
        in_specs=[row, _full(w.shape)],
        out_specs=row,
        out_shape=jax.ShapeDtypeStruct((t, D_MODEL), F32),
        compiler_params=_params("parallel"),
        name="final_norm",
    )(x2, w)


def _pad_lanes(v):
    return jnp.pad(v, (0, LANE - v.shape[0]))[None, :]


def kernel(x, norm_mix_w, w_in, cmp_pos_k, cmp_w1_k, cmp_b1_k, cmp_w2_k, cmp_pos_v, cmp_w1_v,
           cmp_b1_v, cmp_w2_v, nsa_norm_w, ssd_conv_w, ssd_conv_b, ssd_dt_bias, ssd_A_log, ssd_D,
           ssd_norm_w, w_out, norm_ffn_w, w_gate, w_up, ffn_conv_w, ffn_conv_b, w_down, norm_final_w):
    b, s, d = x.shape
    depth = w_in.shape[0]
    t = b * s
    nf = D_FF // FFN_TF
    nsa_consts = _nsa_consts(s)
    x2 = x.reshape(t, d)
    for i in range(depth):
        q, kc, vc, kvs, kvw, gl, z, xbc, dt = _in_proj(x2, norm_mix_w[i][None, :], _pack_w_in(w_in[i]))
        ncp = s // CMP_STRIDE
        kcmp, vcmp = _compress(
            kc.reshape(b, ncp, CMP_STRIDE * KV_WIDTH), vc.reshape(b, ncp, CMP_STRIDE * KV_WIDTH),
            _pack_compress(cmp_pos_k[i], cmp_w1_k[i], cmp_b1_k[i], cmp_w2_k[i]),
            _pack_compress(cmp_pos_v[i], cmp_w1_v[i], cmp_b1_v[i], cmp_w2_v[i]))
        o_attn = _nsa(q.reshape(b, s, -1), kcmp, vcmp, kvs.reshape(b, s, -1), kvw.reshape(b, s, -1),
                      gl.reshape(b, s, -1), nsa_consts, nsa_norm_w[i][None, :])
        dt3 = dt.reshape(b, s, LANE)
        dtt3 = jnp.swapaxes(dt3[:, :, :SSD_HEADS], 1, 2)
        o_ssd = _ssd(z.reshape(b, s, -1), xbc.reshape(b, s, -1), dt3, dtt3,
                     ssd_conv_w[i], ssd_conv_b[i][None, :],
                     _pad_lanes(ssd_dt_bias[i]), ssd_dt_bias[i][:, None],
                     _pad_lanes(ssd_A_log[i]), ssd_A_log[i][:, None],
                     jnp.repeat(ssd_D[i], SSD_HEAD_DIM)[None, :], ssd_norm_w[i][None, :])
        wo = w_out[i].astype(BF16)
        x2 = _out_proj(x2, o_attn.reshape(t, -1), o_ssd.reshape(t, -1), wo[:NSA_WIDTH], wo[NSA_WIDTH:])
        x3 = _ffn(x2.reshape(b, s, d), norm_ffn_w[i][None, :],
                  w_gate[i].reshape(d, nf, FFN_TF).transpose(1, 0, 2).astype(BF16),
                  w_up[i].reshape(d, nf, FFN_TF).transpose(1, 0, 2).astype(BF16),
                  ffn_conv_w[i].reshape(FFN_CONV, nf, FFN_TF).transpose(1, 0, 2),
                  ffn_conv_b[i].reshape(nf, 1, FFN_TF),
                  w_down[i].astype(BF16))
        x2 = x3.reshape(t, d)
    return _final_norm(x2, norm_final_w[None, :]).reshape(b, s, d)
```

```python
import functools
import math

import numpy as np
import jax
import jax.numpy as jnp
from jax import lax
from jax.experimental import pallas as pl
from jax.experimental.pallas import tpu as pltpu

F32 = jnp.float32
BF16 = jnp.bfloat16

D_MODEL = 1024
NSA_HEADS = 8
NSA_KV_GROUPS = 2
NSA_HEAD_DIM = 64
NSA_REP = NSA_HEADS // NSA_KV_GROUPS
NSA_WIDTH = NSA_HEADS * NSA_HEAD_DIM
KV_WIDTH = NSA_KV_GROUPS * NSA_HEAD_DIM
CMP_BLOCK = 32
CMP_STRIDE = 16
CMP_HIDDEN = 128
SLC_BLOCK = 64
SLC_TOP = 16
WIN = 512
SSD_HEADS = 8
SSD_HEAD_DIM = 64
SSD_WIDTH = SSD_HEADS * SSD_HEAD_DIM
SSD_GROUPS = 2
SSD_STATE = 128
SSD_CONV = 4
SSD_CHUNK = 256
SSD_XBC = SSD_WIDTH + 2 * SSD_GROUPS * SSD_STATE
D_FF = 2816
FFN_CONV = 3
RMS_EPS = 1e-6
NEG = -1e30
LOG2E = math.log2(math.e)
FORCE_BONUS = 1e4

IN_SIZES = [NSA_WIDTH, KV_WIDTH, KV_WIDTH, KV_WIDTH, KV_WIDTH, KV_WIDTH, KV_WIDTH,
            3 * NSA_HEADS, SSD_WIDTH, SSD_XBC, SSD_HEADS]
IN_SPLITS = [int(v) for v in np.cumsum(IN_SIZES)[:-1]]

LANE = 128
SUBLANE = 8
HALO = SUBLANE
ATT_TQ = 256
ATT_TK = 256
FFN_TF = 256
FFN_TM = 1024
VMEM_LIMIT = 56 * 1024 * 1024

_NT = (((1,), (1,)), ((), ()))
_TN = (((0,), (0,)), ((), ()))


def _mm(a, b):
    return jnp.dot(a, b, preferred_element_type=F32)


def _mm_nt(a, b):
    return lax.dot_general(a, b, _NT, preferred_element_type=F32)


def _rms(x, w):
    return x * lax.rsqrt(jnp.mean(x * x, axis=-1, keepdims=True) + RMS_EPS) * w


def _silu(x):
    return x * jax.nn.sigmoid(x)


def _params(*sem):
    return pltpu.CompilerParams(dimension_semantics=sem, vmem_limit_bytes=VMEM_LIMIT)


def _full(shape):
    n = len(shape)
    return pl.BlockSpec(shape, lambda *_: (0,) * n)


_PROJ_WIDTHS = (NSA_WIDTH, KV_WIDTH, KV_WIDTH, 2 * KV_WIDTH, 2 * KV_WIDTH, LANE,
                SSD_WIDTH, SSD_XBC, LANE)


def _pack_w_in(w):
    q, kc, vc, ks, vs, kw, vw, gl, z, xbc, dt = jnp.split(w, IN_SPLITS, axis=-1)
    pad = lambda a: jnp.pad(a, ((0, 0), (0, LANE - a.shape[1])))
    return jnp.concatenate([q, kc, vc, ks, vs, kw, vw, pad(gl), z, xbc, pad(dt)], axis=-1).astype(BF16)


def _in_proj_body(x_ref, nw_ref, w_ref, *out_refs):
    h = _rms(x_ref[...], nw_ref[...]).astype(BF16)
    off = 0
    for ref in out_refs:
        n = ref.shape[-1]
        ref[...] = _mm(h, w_ref[:, off:off + n])
        off += n


def _in_proj(x2, nw, w_packed, tm=512):
    t = x2.shape[0]
    ncol = w_packed.shape[1]
    return pl.pallas_call(
        _in_proj_body,
        grid=(t // tm,),
        in_specs=[pl.BlockSpec((tm, D_MODEL), lambda i: (i, 0)),
                  _full((1, D_MODEL)), _full((D_MODEL, ncol))],
        out_specs=[pl.BlockSpec((tm, n), lambda i: (i, 0)) for n in _PROJ_WIDTHS],
        out_shape=[jax.ShapeDtypeStruct((t, n), F32) for n in _PROJ_WIDTHS],
        compiler_params=_params("parallel"),
        name="in_proj",
    )(x2, nw, w_packed)


def _pack_compress(pos, w1, b1, w2):
    half = CMP_BLOCK // 2
    g = NSA_KV_GROUPS
    eye = jnp.eye(g, dtype=F32)
    w1r = w1.reshape(CMP_BLOCK, NSA_HEAD_DIM, CMP_HIDDEN)
    w1e = jnp.einsum('ldn,gh->lgdhn', w1r, eye).reshape(2, half * g * NSA_HEAD_DIM, g * CMP_HIDDEN)
    pose = jnp.broadcast_to(pos[:, None, :], (CMP_BLOCK, g, NSA_HEAD_DIM)).reshape(2, half * g * NSA_HEAD_DIM)
    b1e = jnp.tile(b1, g)[None, :]
    w2e = jnp.einsum('nd,gh->gnhd', w2, eye).reshape(g * CMP_HIDDEN, g * NSA_HEAD_DIM)
    return pose, w1e.astype(BF16), b1e, w2e.astype(BF16)


def _compress_body(kc_ref, vc_ref, pk_ref, wk1_ref, bk1_ref, wk2_ref,
                   pv_ref, wv1_ref, bv1_ref, wv2_ref, ko_ref, vo_ref):
    def one(r_ref, pos_ref, w1_ref, b1_ref, w2_ref, o_ref):
        r = r_ref[0]
        ncp = r.shape[0]
        lo = _mm((r + pos_ref[0:1, :]).astype(BF16), w1_ref[0])
        hi = _mm((r + pos_ref[1:2, :]).astype(BF16), w1_ref[1])
        hid = lo + pltpu.roll(hi, ncp - 1, 0) + b1_ref[...]
        hid = jax.nn.gelu(hid)
        o_ref[0] = _mm(hid.astype(BF16), w2_ref[...])

    one(kc_ref, pk_ref, wk1_ref, bk1_ref, wk2_ref, ko_ref)
    one(vc_ref, pv_ref, wv1_ref, bv1_ref, wv2_ref, vo_ref)


def _compress(kc3, vc3, pk, pv):
    b, ncp, wide = kc3.shape
    blk = pl.BlockSpec((1, ncp, wide), lambda i: (i, 0, 0))
    oblk = pl.BlockSpec((1, ncp, KV_WIDTH), lambda i: (i, 0, 0))
    wspecs = lambda p: [_full(a.shape) for a in p]
    return pl.pallas_call(
        _compress_body,
        grid=(b,),
        in_specs=[blk, blk] + wspecs(pk) + wspecs(pv),
        out_specs=[oblk, oblk],
        out_shape=[jax.ShapeDtypeStruct((b, ncp, KV_WIDTH), F32)] * 2,
        compiler_params=_params("parallel"),
        name="compress",
    )(kc3, vc3, *pk, *pv)


BIAS_ZERO, BIAS_CAUSAL, BIAS_WINLO, BIAS_NONE = range(4)
SEL_ROWS = 32
V_ROWS = 80
assert ATT_TQ == ATT_TK and WIN == 2 * ATT_TK and KV_WIDTH == LANE and 2 * NSA_HEAD_DIM == LANE


def _nsa_consts(s):
    ncp = s // CMP_STRIDE
    ns = s // SLC_BLOCK
    assert ns <= SEL_ROWS
    c0 = np.arange(ncp)[None, :] * CMP_STRIDE
    jj = np.arange(SEL_ROWS)[:, None]
    ovl_t = (c0 < jj * SLC_BLOCK + SLC_BLOCK) & (c0 + CMP_BLOCK > jj * SLC_BLOCK) & (jj < ns)
    onehot = np.zeros((s, LANE), np.float32)
    onehot[np.arange(s), NSA_HEAD_DIM + np.arange(s) // SLC_BLOCK] = 1.0
    k = np.arange(ATT_TK)[:, None]
    t = np.arange(ATT_TQ)[None, :]
    bias = np.stack([np.zeros((ATT_TK, ATT_TQ)),
                     np.where(k <= t, 0.0, NEG),
                     np.where(k > t, 0.0, NEG),
                     np.full((ATT_TK, ATT_TQ), NEG)]).astype(np.float32)
    return jnp.asarray(ovl_t.astype(np.float32), BF16), jnp.asarray(onehot), jnp.asarray(bias)


def _nsa_body(q_ref, kcmp_ref, vcmp_ref, kvs_ref, kvw_ref, gl_ref, ovl_ref, onehot_ref, bias_ref,
              nw_ref, o_ref, kaug_ref, vaug_ref, acc_ref, m_ref, *, seq):
    tq, tk, hd, rep = ATT_TQ, ATT_TK, NSA_HEAD_DIM, NSA_REP
    j = pl.program_id(1)
    ntop = min(SLC_TOP, seq // SLC_BLOCK)
    ncp = kcmp_ref.shape[1]
    nq = rep * tq

    @pl.when(j == 0)
    def _():
        lane = lax.broadcasted_iota(jnp.int32, (tk, LANE), 1)
        ones_col = jnp.where(lane == hd, 1.0, 0.0)
        for br, kv_ref in enumerate((kvs_ref, kvw_ref)):
            def chunk(c, carry, br=br, kv_ref=kv_ref):
                r0 = pl.multiple_of(c * tk, tk)
                k128 = kv_ref[0, pl.ds(r0, tk), 0:KV_WIDTH]
                v128 = kv_ref[0, pl.ds(r0, tk), KV_WIDTH:2 * KV_WIDTH]
                extra = onehot_ref[pl.ds(r0, tk), :] if br == 0 else 0.0
                for g in range(NSA_KV_GROUPS):
                    kk = k128 if g == 0 else pltpu.roll(k128, hd, 1)
                    vv = v128 if g == 0 else pltpu.roll(v128, hd, 1)
                    kaug_ref[2 * br + g, pl.ds(r0, tk), :] = jnp.where(lane < hd, kk, extra).astype(BF16)
                    vt = jnp.where(lane < hd, vv, ones_col).T
                    vaug_ref[2 * br + g, :, pl.ds(r0, tk)] = vt[0:V_ROWS].astype(BF16)
                return carry
            lax.fori_loop(0, seq // tk, chunk, 0)

    def scores(kv_slot, kt, q_all):
        k0 = pl.multiple_of(kt * tk, tk)
        return _mm_nt(kaug_ref[kv_slot, pl.ds(k0, tk), :], q_all)

    def absorb(items):
        stats = []
        for chain, _, _, s_t in items:
            m_old = m_ref[chain]
            m_new = jnp.maximum(m_old, jnp.max(s_t, axis=0, keepdims=True))
            stats.append((m_old, m_new, jnp.exp2(s_t - m_new).astype(BF16)))
        pvs = [_mm(vaug_ref[kv_slot, :, pl.ds(pl.multiple_of(kt * tk, tk), tk)], p)
               for (_, kv_slot, kt, _), (_, _, p) in zip(items, stats)]
        for (chain, _, _, _), (m_old, m_new, _), pv in zip(items, stats, pvs):
            acc_ref[chain] = jnp.exp2(m_old - m_new) * acc_ref[chain] + pv
            m_ref[chain] = m_new

    def run(tiles):
        items = [(chain, kv_slot, kt, scores(kv_slot, kt, q) if bias is None else scores(kv_slot, kt, q) + bias)
                 for chain, kv_slot, kt, q, bias in tiles]
        for n in range(0, len(items), 2):
            absorb(items[n:n + 2])

    def finish(chain):
        acc = acc_ref[chain]
        return acc[0:hd] / acc[hd:hd + 1]

    lane_q = lax.broadcasted_iota(jnp.int32, (tq, LANE), 1)
    lane_c = lax.broadcasted_iota(jnp.int32, (ncp, LANE), 1)
    vc_t = vcmp_ref[0].T
    j_sub = lax.broadcasted_iota(jnp.int32, (SEL_ROWS, tq), 0)
    j8 = lax.broadcasted_iota(jnp.int32, (SUBLANE, tq), 0)
    ngrp = SEL_ROWS // SUBLANE
    groups = range(NSA_KV_GROUPS)

    def prepare(qt):
        t0 = (2 * j + qt) * tq
        qv = q_ref[0, qt * tq:(qt + 1) * tq, :]
        t_lane = t0 + (lax.broadcasted_iota(jnp.int32, (ncp, nq), 1) & (tq - 1))
        cmask = (lax.broadcasted_iota(jnp.int32, (ncp, nq), 0) * CMP_STRIDE + (CMP_BLOCK - 1)) <= t_lane
        cur = jnp.right_shift(t0 + lax.broadcasted_iota(jnp.int32, (SEL_ROWS, tq), 1),
                              int(math.log2(SLC_BLOCK)))
        bonus = jnp.where((j_sub == 0) | (j_sub == cur) | (j_sub == cur - 1), FORCE_BONUS, 0.0)
        valid = j_sub <= cur
        q_plain, q_slc, o_cmp = [], [], []
        for g in groups:
            heads = []
            for r in range(rep):
                hidx = g * rep + r
                c128 = qv[:, (hidx // 2) * LANE:(hidx // 2 + 1) * LANE]
                if hidx % 2:
                    c128 = pltpu.roll(c128, hd, 1)
                heads.append(jnp.where(lane_q < hd, c128 * (LOG2E * hd ** -0.5), 0.0))
            q_plain.append(jnp.concatenate(heads, axis=0).astype(BF16))

            kc = kcmp_ref[0]
            if g:
                kc = pltpu.roll(kc, hd, 1)
            kc = jnp.where(lane_c < hd, kc, 0.0).astype(BF16)
            s_t = jnp.where(cmask, _mm_nt(kc, q_plain[g]), NEG)
            e = jnp.exp2(s_t - jnp.max(s_t, axis=0, keepdims=True))
            p = jnp.where(cmask, e / jnp.sum(e, axis=0, keepdims=True), 0.0).astype(BF16)
            o_cmp.append(_mm(vc_t[g * hd:(g + 1) * hd].astype(BF16), p))
            imp4 = _mm(ovl_ref[...], p)
            imp = imp4[:, 0:tq]
            for r in range(1, rep):
                imp = imp + imp4[:, r * tq:(r + 1) * tq]

            score = jnp.where(valid, imp + bonus, NEG)
            grp = [score[SUBLANE * a:SUBLANE * (a + 1)] for a in range(ngrp)]
            rank = [jnp.zeros((SUBLANE, tq), F32) for _ in range(ngrp)]
            for k in range(SEL_ROWS):
                sk = jnp.broadcast_to(score[k:k + 1, :], (SUBLANE, tq))
                for a in range(ngrp):
                    if k < SUBLANE * a:
                        inc = jnp.where(sk >= grp[a], 1.0, 0.0)
                    elif k >= SUBLANE * (a + 1):
                        inc = jnp.where(sk > grp[a], 1.0, 0.0)
                    else:
                        inc = jnp.where(j8 > k - SUBLANE * a, jnp.where(sk >= grp[a], 1.0, 0.0),
                                        jnp.where(sk > grp[a], 1.0, 0.0))
                    rank[a] = rank[a] + inc
            bias_rows = [jnp.where(rk < ntop, 0.0, NEG) for rk in rank]
            b128 = jnp.concatenate([jnp.zeros((hd, tq), F32)] + bias_rows
                                   + [jnp.zeros((LANE - hd - SEL_ROWS, tq), F32)], axis=0)
            bq = b128.T
            q_slc.append(jnp.concatenate([h + bq for h in heads], axis=0).astype(BF16))
        return q_plain, q_slc, o_cmp

    prep = [prepare(qt) for qt in range(2)]
    q_plain = [p[0] for p in prep]
    q_slc = [p[1] for p in prep]

    chain = lambda br, qt, g: (br * 2 + qt) * NSA_KV_GROUPS + g
    m_ref[...] = jnp.full(m_ref.shape, NEG, F32)
    acc_ref[...] = jnp.zeros(acc_ref.shape, F32)
    tiled = lambda kind: jnp.concatenate([bias_ref[kind]] * rep, axis=1)

    def slc_step(t, carry):
        run([(chain(0, qt, g), g, 2 * t + u, q_slc[qt][g], None)
             for u in range(2) for qt in range(2) for g in groups])
        return carry

    lax.fori_loop(0, j, slc_step, 0)
    causal = tiled(BIAS_CAUSAL)
    run([(chain(0, 0, g), g, 2 * j, q_slc[0][g], causal) for g in groups]
        + [(chain(0, 1, g), g, 2 * j, q_slc[1][g], None) for g in groups]
        + [(chain(0, 1, g), g, 2 * j + 1, q_slc[1][g], causal) for g in groups])
    for qt in range(2):
        tiles = []
        for step, kind in enumerate((BIAS_WINLO, BIAS_ZERO, BIAS_CAUSAL)):
            kt = 2 * j + qt - 2 + step
            if step == 2:
                bias = causal
            elif step == 1 and qt == 1:
                bias = None
            else:
                bias = tiled(jnp.where(kt >= 0, kind, BIAS_NONE))
            tiles += [(chain(1, qt, g), 2 + g, jnp.maximum(kt, 0), q_plain[qt][g], bias) for g in groups]
        run(tiles)

    for qt in range(2):
        gates_t = jax.nn.sigmoid(gl_ref[0, qt * tq:(qt + 1) * tq, :]).T
        o_cmp = prep[qt][2]
        outs_t = []
        for g in groups:
            o_slc = finish(chain(0, qt, g))
            o_win = finish(chain(1, qt, g))
            for r in range(rep):
                c = (g * rep + r) * 3
                sl = slice(r * tq, (r + 1) * tq)
                outs_t.append(gates_t[c:c + 1] * o_cmp[g][:, sl] + gates_t[c + 1:c + 2] * o_slc[:, sl]
                              + gates_t[c + 2:c + 3] * o_win[:, sl])
        o_ref[0, qt * tq:(qt + 1) * tq, :] = _rms(jnp.concatenate(outs_t, axis=0).T, nw_ref[...])


def _nsa(q3, kcmp, vcmp, kvs3, kvw3, gl3, consts, nw):
    b, s, _ = q3.shape
    tq = ATT_TQ
    ncp = kcmp.shape[1]
    row = lambda w: pl.BlockSpec((1, 2 * tq, w), lambda bi, i: (bi, i, 0))
    per_b = lambda n, w: pl.BlockSpec((1, n, w), lambda bi, i: (bi, 0, 0))
    return pl.pallas_call(
        functools.partial(_nsa_body, seq=s),
        grid=(b, s // (2 * tq)),
        in_specs=[row(NSA_WIDTH), per_b(ncp, KV_WIDTH), per_b(ncp, KV_WIDTH),
                  per_b(s, 2 * KV_WIDTH), per_b(s, 2 * KV_WIDTH), row(LANE)]
                 + [_full(c.shape) for c in consts] + [_full((1, NSA_WIDTH))],
        out_specs=row(NSA_WIDTH),
        out_shape=jax.ShapeDtypeStruct((b, s, NSA_WIDTH), F32),
        scratch_shapes=[pltpu.VMEM((2 * NSA_KV_GROUPS, s, LANE), BF16),
                        pltpu.VMEM((2 * NSA_KV_GROUPS, V_ROWS, s), BF16),
                        pltpu.VMEM((4 * NSA_KV_GROUPS, V_ROWS, NSA_REP * tq), F32),
                        pltpu.VMEM((4 * NSA_KV_GROUPS, 1, NSA_REP * tq), F32)],
        compiler_params=_params("parallel", "arbitrary"),
        name="nsa_attn",
    )(q3, kcmp, vcmp, kvs3, kvw3, gl3, *consts, nw)


def _shifted_rows(x, halo, sh):
    r = pltpu.roll(x, sh, 0)
    row = lax.broadcasted_iota(jnp.int32, (HALO, x.shape[1]), 0)
    top = jnp.where(row < sh, pltpu.roll(halo, sh, 0), r[0:HALO])
    return jnp.concatenate([top, r[HALO:]], axis=0)


def _ssd_body(z_ref, xbc_ref, dt_ref, dtt_ref, cw_ref, cb_ref, dtb_row_ref, dtb_col_ref,
              alog_row_ref, alog_col_ref, d_ref, nw_ref, y_ref, state_ref, halo_ref):
    L = SSD_CHUNK
    hd, nst = SSD_HEAD_DIM, SSD_STATE
    rep = SSD_HEADS // SSD_GROUPS
    gw = rep * hd

    @pl.when(pl.program_id(1) == 0)
    def _():
        state_ref[...] = jnp.zeros_like(state_ref)
        halo_ref[...] = jnp.zeros_like(halo_ref)

    xr = xbc_ref[0]
    halo = halo_ref[...]
    conv = cb_ref[...] + cw_ref[SSD_CONV - 1:SSD_CONV, :] * xr
    for sh in range(1, SSD_CONV):
        conv = conv + cw_ref[SSD_CONV - 1 - sh:SSD_CONV - sh, :] * _shifted_rows(xr, halo, sh)
    halo_ref[...] = xr[L - HALO:L, :]
    xc = _silu(conv)
    xs = xc[:, :SSD_WIDTH]
    bm = xc[:, SSD_WIDTH:SSD_WIDTH + SSD_GROUPS * nst]
    cm = xc[:, SSD_WIDTH + SSD_GROUPS * nst:]

    dt_col = jax.nn.softplus(dt_ref[0] + dtb_row_ref[...])
    a_col = dt_col * (-jnp.exp(alog_row_ref[...]))
    i0 = lax.broadcasted_iota(jnp.int32, (L, L), 0)
    i1 = lax.broadcasted_iota(jnp.int32, (L, L), 1)
    tri = i0 >= i1
    acs_col = jnp.dot(jnp.where(tri, 1.0, 0.0), a_col, precision=lax.Precision.HIGHEST,
                      preferred_element_type=F32)
    dt_row = jax.nn.softplus(dtt_ref[0] + dtb_col_ref[...])
    a_row = dt_row * (-jnp.exp(alog_col_ref[...]))
    acs_row = jnp.dot(a_row, jnp.where(i0 <= i1, 1.0, 0.0), precision=lax.Precision.HIGHEST,
                      preferred_element_type=F32)
    acs_last = acs_col[L - 1:L, :]

    ys = []
    for g in range(SSD_GROUPS):
        bg = bm[:, g * nst:(g + 1) * nst].astype(BF16)
        cg = cm[:, g * nst:(g + 1) * nst].astype(BF16)
        cb = _mm_nt(cg, bg)
        state = state_ref[g]
        y_off = _mm(cg, state.astype(BF16))
        sx, cdec = [], []
        for r in range(rep):
            h = g * rep + r
            col = acs_col[:, h:h + 1]
            seg = col - acs_row[h:h + 1, :]
            xh = xs[:, h * hd:(h + 1) * hd]
            xdt = xh * dt_col[:, h:h + 1]
            y = _mm(jnp.where(tri, cb * jnp.exp(seg), 0.0).astype(BF16), xdt.astype(BF16))
            y = y + jnp.exp(col) * y_off[:, r * hd:(r + 1) * hd] + d_ref[:, h * hd:(h + 1) * hd] * xh
            ys.append(y)
            last = acs_last[:, h:h + 1]
            sx.append(xdt * jnp.exp(last - col))
            cdec.append(jnp.broadcast_to(jnp.exp(last), (1, hd)))
        sxg = jnp.concatenate(sx, axis=1).astype(BF16)
        new = lax.dot_general(bg, sxg, _TN, preferred_element_type=F32)
        state_ref[g] = state * jnp.concatenate(cdec, axis=1) + new
    y = jnp.concatenate(ys, axis=1) * _silu(z_ref[0])
    outs = []
    for g in range(SSD_GROUPS):
        outs.append(_rms(y[:, g * gw:(g + 1) * gw], nw_ref[:, g * gw:(g + 1) * gw]))
    y_ref[0] = jnp.concatenate(outs, axis=1)


def _ssd(z3, xbc3, dt3, dtt3, cw, cb, dtb_row, dtb_col, alog_row, alog_col, d_exp, nw):
    b, s, _ = z3.shape
    L = SSD_CHUNK
    row = lambda w: pl.BlockSpec((1, L, w), lambda bi, c: (bi, c, 0))
    small = [cw, cb, dtb_row, dtb_col, alog_row, alog_col, d_exp, nw]
    return pl.pallas_call(
        _ssd_body,
        grid=(b, s // L),
        in_specs=[row(SSD_WIDTH), row(SSD_XBC), row(LANE),
                  pl.BlockSpec((1, SSD_HEADS, L), lambda bi, c: (bi, 0, c))]
                 + [_full(a.shape) for a in small],
        out_specs=row(SSD_WIDTH),
        out_shape=jax.ShapeDtypeStruct((b, s, SSD_WIDTH), F32),
        scratch_shapes=[pltpu.VMEM((SSD_GROUPS, SSD_STATE, SSD_WIDTH // SSD_GROUPS), F32),
                        pltpu.VMEM((HALO, SSD_XBC), F32)],
        compiler_params=_params("parallel", "arbitrary"),
        name="ssd",
    )(z3, xbc3, dt3, dtt3, *small)


def _out_proj_body(x_ref, a_ref, s_ref, wa_ref, ws_ref, o_ref):
    o_ref[...] = (x_ref[...] + _mm(a_ref[...].astype(BF16), wa_ref[...])
                  + _mm(s_ref[...].astype(BF16), ws_ref[...]))


def _out_proj(x2, a2, s2, wa, ws, tm=512):
    t = x2.shape[0]
    row = lambda w: pl.BlockSpec((tm, w), lambda i: (i, 0))
    return pl.pallas_call(
        _out_proj_body,
        grid=(t // tm,),
        in_specs=[row(D_MODEL), row(NSA_WIDTH), row(SSD_WIDTH), _full(wa.shape), _full(ws.shape)],
        out_specs=row(D_MODEL),
        out_shape=jax.ShapeDtypeStruct((t, D_MODEL), F32),
        compiler_params=_params("parallel"),
        name="out_proj",
    )(x2, a2, s2, wa, ws)


def _ffn_body(x_ref, nw_ref, wg_ref, wu_ref, cw_ref, cb_ref, wd_ref, o_ref, halo_ref, act_ref):
    tm = x_ref.shape[1]
    nf = wg_ref.shape[1] // FFN_TF

    @pl.when(pl.program_id(1) == 0)
    def _():
        halo_ref[...] = jnp.zeros_like(halo_ref)

    x = x_ref[0]
    h = _rms(x, nw_ref[...]).astype(BF16)
    for f in range(nf):
        gp = _mm(h, wg_ref[:, f * FFN_TF:(f + 1) * FFN_TF])
        up = _mm(h, wu_ref[:, f * FFN_TF:(f + 1) * FFN_TF])
        halo = halo_ref[f]
        cw = cw_ref[f]
        gate = cb_ref[f] + cw[FFN_CONV - 1:FFN_CONV, :] * gp
        for sh in range(1, FFN_CONV):
            gate = gate + cw[FFN_CONV - 1 - sh:FFN_CONV - sh, :] * _shifted_rows(gp, halo, sh)
        halo_ref[f] = gp[tm - HALO:tm, :]
        act_ref[:, f * FFN_TF:(f + 1) * FFN_TF] = (_silu(gate) * up).astype(BF16)
    o_ref[0] = x + _mm(act_ref[...], wd_ref[...])


def _ffn(x3, nw, wg, wu, cw, cb, wd, tm=FFN_TM):
    b, s, _ = x3.shape
    nf = wg.shape[1] // FFN_TF
    row = pl.BlockSpec((1, tm, D_MODEL), lambda bi, i: (bi, i, 0))
    once = lambda a: pl.BlockSpec(a.shape, lambda *_: (0,) * a.ndim, pipeline_mode=pl.Buffered(1))
    return pl.pallas_call(
        _ffn_body,
        grid=(b, s // tm),
        in_specs=[row, _full(nw.shape), once(wg), once(wu), _full(cw.shape), _full(cb.shape), once(wd)],
        out_specs=row,
        out_shape=jax.ShapeDtypeStruct((b, s, D_MODEL), F32),
        scratch_shapes=[pltpu.VMEM((nf, HALO, FFN_TF), F32), pltpu.VMEM((tm, D_FF), BF16)],
        input_output_aliases={0: 0},
        compiler_params=_params("parallel", "arbitrary"),
        name="conv_ffn",
    )(x3, nw, wg, wu, cw, cb, wd)


def _final_norm_body(x_ref, w_ref, o_ref):
    o_ref[...] = _rms(x_ref[...], w_ref[...])


def _final_norm(x2, w, tm=1024):
    t = x2.shape[0]
    row = pl.BlockSpec((tm, D_MODEL), lambda i: (i, 0))
    return pl.pallas_call(
        _final_norm_body,
        grid=(t // tm,),
        in_specs=[row, _full(w.shape)],
        out_specs=row,
        out_shape=jax.ShapeDtypeStruct((t, D_MODEL), F32),
        compiler_params=_params("parallel"),
        name="final_norm",
    )(x2, w)


def _pad_lanes(v):
    return jnp.pad(v, (0, LANE - v.shape[0]))[None, :]


def kernel(x, norm_mix_w, w_in, cmp_pos_k, cmp_w1_k, cmp_b1_k, cmp_w2_k, cmp_pos_v, cmp_w1_v,
           cmp_b1_v, cmp_w2_v, nsa_norm_w, ssd_conv_w, ssd_conv_b, ssd_dt_bias, ssd_A_log, ssd_D,
           ssd_norm_w, w_out, norm_ffn_w, w_gate, w_up, ffn_conv_w, ffn_conv_b, w_down, norm_final_w):
    b, s, d = x.shape
    depth = w_in.shape[0]
    t = b * s
    nf = D_FF // FFN_TF
    nsa_consts = _nsa_consts(s)
    x2 = x.reshape(t, d)
    for i in range(depth):
        q, kc, vc, kvs, kvw, gl, z, xbc, dt = _in_proj(x2, norm_mix_w[i][None, :], _pack_w_in(w_in[i]))
        ncp = s // CMP_STRIDE
        kcmp, vcmp = _compress(
            kc.reshape(b, ncp, CMP_STRIDE * KV_WIDTH), vc.reshape(b, ncp, CMP_STRIDE * KV_WIDTH),
            _pack_compress(cmp_pos_k[i], cmp_w1_k[i], cmp_b1_k[i], cmp_w2_k[i]),
            _pack_compress(cmp_pos_v[i], cmp_w1_v[i], cmp_b1_v[i], cmp_w2_v[i]))
        o_attn = _nsa(q.reshape(b, s, -1), kcmp, vcmp, kvs.reshape(b, s, -1), kvw.reshape(b, s, -1),
                      gl.reshape(b, s, -1), nsa_consts, nsa_norm_w[i][None, :])
        dt3 = dt.reshape(b, s, LANE)
        dtt3 = jnp.swapaxes(dt3[:, :, :SSD_HEADS], 1, 2)
        o_ssd = _ssd(z.reshape(b, s, -1), xbc.reshape(b, s, -1), dt3, dtt3,
                     ssd_conv_w[i], ssd_conv_b[i][None, :],
                     _pad_lanes(ssd_dt_bias[i]), ssd_dt_bias[i][:, None],
                     _pad_lanes(ssd_A_log[i]), ssd_A_log[i][:, None],
                     jnp.repeat(ssd_D[i], SSD_HEAD_DIM)[None, :], ssd_norm_w[i][None, :])
        wo = w_out[i].astype(BF16)
        x2 = _out_proj(x2, o_attn.reshape(t, -1), o_ssd.reshape(t, -1), wo[:NSA_WIDTH], wo[NSA_WIDTH:])
        x3 = _ffn(x2.reshape(b, s, d), norm_ffn_w[i][None, :],
                  w_gate[i].astype(BF16), w_up[i].astype(BF16),
                  ffn_conv_w[i].reshape(FFN_CONV, nf, FFN_TF).transpose(1, 0, 2),
                  ffn_conv_b[i].reshape(nf, 1, FFN_TF),
                  w_down[i].astype(BF16))
        x2 = x3.reshape(t, d)
    return _final_norm(x2, norm_final_w[None, :]).reshape(b, s, d)
```

```python
import functools
import math

import numpy as np
import jax
import jax.numpy as jnp
from jax import lax
from jax.experimental import pallas as pl
from jax.experimental.pallas import tpu as pltpu

F32 = jnp.float32
BF16 = jnp.bfloat16

D_MODEL = 1024
NSA_HEADS = 8
NSA_KV_GROUPS = 2
NSA_HEAD_DIM = 64
NSA_REP = NSA_HEADS // NSA_KV_GROUPS
NSA_WIDTH = NSA_HEADS * NSA_HEAD_DIM
KV_WIDTH = NSA_KV_GROUPS * NSA_HEAD_DIM
CMP_BLOCK = 32
CMP_STRIDE = 16
CMP_HIDDEN = 128
SLC_BLOCK = 64
SLC_TOP = 16
WIN = 512
SSD_HEADS = 8
SSD_HEAD_DIM = 64
SSD_WIDTH = SSD_HEADS * SSD_HEAD_DIM
SSD_GROUPS = 2
SSD_STATE = 128
SSD_CONV = 4
SSD_CHUNK = 256
SSD_XBC = SSD_WIDTH + 2 * SSD_GROUPS * SSD_STATE
D_FF = 2816
FFN_CONV = 3
RMS_EPS = 1e-6
NEG = -1e30
LOG2E = math.log2(math.e)
FORCE_BONUS = 1e4

IN_SIZES = [NSA_WIDTH, KV_WIDTH, KV_WIDTH, KV_WIDTH, KV_WIDTH, KV_WIDTH, KV_WIDTH,
            3 * NSA_HEADS, SSD_WIDTH, SSD_XBC, SSD_HEADS]
IN_SPLITS = [int(v) for v in np.cumsum(IN_SIZES)[:-1]]

LANE = 128
SUBLANE = 8
HALO = SUBLANE
ATT_TQ = 256
ATT_TK = 256
FFN_TF = 256
FFN_TM = 1024
VMEM_LIMIT = 56 * 1024 * 1024

_NT = (((1,), (1,)), ((), ()))
_TN = (((0,), (0,)), ((), ()))


def _mm(a, b):
    return jnp.dot(a, b, preferred_element_type=F32)


def _mm_nt(a, b):
    return lax.dot_general(a, b, _NT, preferred_element_type=F32)


def _rms(x, w):
    return x * lax.rsqrt(jnp.mean(x * x, axis=-1, keepdims=True) + RMS_EPS) * w


def _silu(x):
    return x * jax.nn.sigmoid(x)


def _params(*sem):
    return pltpu.CompilerParams(dimension_semantics=sem, vmem_limit_bytes=VMEM_LIMIT)


def _full(shape):
    n = len(shape)
    return pl.BlockSpec(shape, lambda *_: (0,) * n)


_PROJ_WIDTHS = (NSA_WIDTH, KV_WIDTH, KV_WIDTH, 2 * KV_WIDTH, 2 * KV_WIDTH, LANE, LANE,
                SSD_WIDTH, SSD_XBC)


def _pack_w_in(w):
    q, kc, vc, ks, vs, kw, vw, gl, z, xbc, dt = jnp.split(w, IN_SPLITS, axis=-1)
    pad = lambda a: jnp.pad(a, ((0, 0), (0, LANE - a.shape[1])))
    return jnp.concatenate([q, kc, vc, ks, vs, kw, vw, pad(gl), pad(dt), z, xbc], axis=-1).astype(BF16)


def _in_proj_body(x_ref, nw_ref, w_ref, *out_refs):
    h = _rms(x_ref[...], nw_ref[...]).astype(BF16)
    off = 0
    pending = list(out_refs)
    while pending:
        n_out = 2 if (pending[0].shape[-1] == LANE and len(pending) > 1 and pending[1].shape[-1] == LANE) else 1
        refs, pending = pending[:n_out], pending[n_out:]
        n = sum(r.shape[-1] for r in refs)
        res = _mm(h, w_ref[:, off:off + n])
        col = 0
        for r in refs:
            r[...] = res[:, col:col + r.shape[-1]]
            col += r.shape[-1]
        off += n


def _in_proj(x2, nw, w_packed, tm=512):
    t = x2.shape[0]
    ncol = w_packed.shape[1]
    return pl.pallas_call(
        _in_proj_body,
        grid=(t // tm,),
        in_specs=[pl.BlockSpec((tm, D_MODEL), lambda i: (i, 0)),
                  _full((1, D_MODEL)), _full((D_MODEL, ncol))],
        out_specs=[pl.BlockSpec((tm, n), lambda i: (i, 0)) for n in _PROJ_WIDTHS],
        out_shape=[jax.ShapeDtypeStruct((t, n), F32) for n in _PROJ_WIDTHS],
        compiler_params=_params("parallel"),
        name="in_proj",
    )(x2, nw, w_packed)


def _pack_compress(pos, w1, b1, w2):
    half = CMP_BLOCK // 2
    g = NSA_KV_GROUPS
    eye = jnp.eye(g, dtype=F32)
    w1r = w1.reshape(CMP_BLOCK, NSA_HEAD_DIM, CMP_HIDDEN)
    w1e = jnp.einsum('ldn,gh->lgdhn', w1r, eye).reshape(2, half * g * NSA_HEAD_DIM, g * CMP_HIDDEN)
    pose = jnp.broadcast_to(pos[:, None, :], (CMP_BLOCK, g, NSA_HEAD_DIM)).reshape(2, half * g * NSA_HEAD_DIM)
    b1e = jnp.tile(b1, g)[None, :]
    w2e = jnp.einsum('nd,gh->gnhd', w2, eye).reshape(g * CMP_HIDDEN, g * NSA_HEAD_DIM)
    return pose, w1e.astype(BF16), b1e, w2e.astype(BF16)


def _compress_body(kc_ref, vc_ref, pk_ref, wk1_ref, bk1_ref, wk2_ref,
                   pv_ref, wv1_ref, bv1_ref, wv2_ref, ko_ref, vo_ref):
    def one(r_ref, pos_ref, w1_ref, b1_ref, w2_ref, o_ref):
        r = r_ref[0]
        ncp = r.shape[0]
        lo = _mm((r + pos_ref[0:1, :]).astype(BF16), w1_ref[0])
        hi = _mm((r + pos_ref[1:2, :]).astype(BF16), w1_ref[1])
        hid = lo + pltpu.roll(hi, ncp - 1, 0) + b1_ref[...]
        hid = jax.nn.gelu(hid)
        o_ref[0] = _mm(hid.astype(BF16), w2_ref[...])

    one(kc_ref, pk_ref, wk1_ref, bk1_ref, wk2_ref, ko_ref)
    one(vc_ref, pv_ref, wv1_ref, bv1_ref, wv2_ref, vo_ref)


def _compress(kc3, vc3, pk, pv):
    b, ncp, wide = kc3.shape
    blk = pl.BlockSpec((1, ncp, wide), lambda i: (i, 0, 0))
    oblk = pl.BlockSpec((1, ncp, KV_WIDTH), lambda i: (i, 0, 0))
    wspecs = lambda p: [_full(a.shape) for a in p]
    return pl.pallas_call(
        _compress_body,
        grid=(b,),
        in_specs=[blk, blk] + wspecs(pk) + wspecs(pv),
        out_specs=[oblk, oblk],
        out_shape=[jax.ShapeDtypeStruct((b, ncp, KV_WIDTH), F32)] * 2,
        compiler_params=_params("parallel"),
        name="compress",
    )(kc3, vc3, *pk, *pv)


BIAS_ZERO, BIAS_CAUSAL, BIAS_WINLO, BIAS_NONE = range(4)
SEL_ROWS = 32
V_ROWS = 80
assert ATT_TQ == ATT_TK and WIN == 2 * ATT_TK and KV_WIDTH == LANE and 2 * NSA_HEAD_DIM == LANE


def _nsa_consts(s):
    ncp = s // CMP_STRIDE
    ns = s // SLC_BLOCK
    assert ns <= SEL_ROWS
    c0 = np.arange(ncp)[None, :] * CMP_STRIDE
    jj = np.arange(SEL_ROWS)[:, None]
    ovl_t = (c0 < jj * SLC_BLOCK + SLC_BLOCK) & (c0 + CMP_BLOCK > jj * SLC_BLOCK) & (jj < ns)
    onehot = np.zeros((s, LANE), np.float32)
    onehot[np.arange(s), NSA_HEAD_DIM + np.arange(s) // SLC_BLOCK] = 1.0
    k = np.arange(ATT_TK)[:, None]
    t = np.arange(ATT_TQ)[None, :]
    bias = np.stack([np.zeros((ATT_TK, ATT_TQ)),
                     np.where(k <= t, 0.0, NEG),
                     np.where(k > t, 0.0, NEG),
                     np.full((ATT_TK, ATT_TQ), NEG)]).astype(np.float32)
    return jnp.asarray(ovl_t.astype(np.float32), BF16), jnp.asarray(onehot), jnp.asarray(bias)


def _nsa_body(q_ref, kcmp_ref, vcmp_ref, kvs_ref, kvw_ref, gl_ref, ovl_ref, onehot_ref, bias_ref,
              nw_ref, o_ref, kaug_ref, vaug_ref, acc_ref, m_ref, *, seq):
    tq, tk, hd, rep = ATT_TQ, ATT_TK, NSA_HEAD_DIM, NSA_REP
    j = pl.program_id(1)
    ntop = min(SLC_TOP, seq // SLC_BLOCK)
    ncp = kcmp_ref.shape[1]
    nq = rep * tq

    @pl.when(j == 0)
    def _():
        lane = lax.broadcasted_iota(jnp.int32, (tk, LANE), 1)
        ones_col = jnp.where(lane == hd, 1.0, 0.0)
        for br, kv_ref in enumerate((kvs_ref, kvw_ref)):
            def chunk(c, carry, br=br, kv_ref=kv_ref):
                r0 = pl.multiple_of(c * tk, tk)
                k128 = kv_ref[0, pl.ds(r0, tk), 0:KV_WIDTH]
                v128 = kv_ref[0, pl.ds(r0, tk), KV_WIDTH:2 * KV_WIDTH]
                extra = onehot_ref[pl.ds(r0, tk), :] if br == 0 else 0.0
                for g in range(NSA_KV_GROUPS):
                    kk = k128 if g == 0 else pltpu.roll(k128, hd, 1)
                    vv = v128 if g == 0 else pltpu.roll(v128, hd, 1)
                    kaug_ref[2 * br + g, pl.ds(r0, tk), :] = jnp.where(lane < hd, kk, extra).astype(BF16)
                    vt = jnp.where(lane < hd, vv, ones_col).T
                    vaug_ref[2 * br + g, :, pl.ds(r0, tk)] = vt[0:V_ROWS].astype(BF16)
                return carry
            lax.fori_loop(0, seq // tk, chunk, 0)

    def scores(kv_slot, kt, q_all):
        k0 = pl.multiple_of(kt * tk, tk)
        return _mm_nt(kaug_ref[kv_slot, pl.ds(k0, tk), :], q_all)

    def absorb(items):
        stats = []
        for chain, _, _, s_t in items:
            m_old = m_ref[chain]
            m_new = jnp.maximum(m_old, jnp.max(s_t, axis=0, keepdims=True))
            stats.append((m_old, m_new, jnp.exp2(s_t - m_new).astype(BF16)))
        pvs = [_mm(vaug_ref[kv_slot, :, pl.ds(pl.multiple_of(kt * tk, tk), tk)], p)
               for (_, kv_slot, kt, _), (_, _, p) in zip(items, stats)]
        for (chain, _, _, _), (m_old, m_new, _), pv in zip(items, stats, pvs):
            acc_ref[chain] = jnp.exp2(m_old - m_new) * acc_ref[chain] + pv
            m_ref[chain] = m_new

    def run(tiles):
        items = [(chain, kv_slot, kt, scores(kv_slot, kt, q) if bias is None else scores(kv_slot, kt, q) + bias)
                 for chain, kv_slot, kt, q, bias in tiles]
        for n in range(0, len(items), 2):
            absorb(items[n:n + 2])

    def finish(chain):
        acc = acc_ref[chain]
        return acc[0:hd] / acc[hd:hd + 1]

    lane_q = lax.broadcasted_iota(jnp.int32, (tq, LANE), 1)
    lane_c = lax.broadcasted_iota(jnp.int32, (ncp, LANE), 1)
    vc_t = vcmp_ref[0].T
    j_sub = lax.broadcasted_iota(jnp.int32, (SEL_ROWS, tq), 0)
    j8 = lax.broadcasted_iota(jnp.int32, (SUBLANE, tq), 0)
    ngrp = SEL_ROWS // SUBLANE
    groups = range(NSA_KV_GROUPS)

    def prepare(qt):
        t0 = (2 * j + qt) * tq
        qv = q_ref[0, qt * tq:(qt + 1) * tq, :]
        t_lane = t0 + (lax.broadcasted_iota(jnp.int32, (ncp, nq), 1) & (tq - 1))
        cmask = (lax.broadcasted_iota(jnp.int32, (ncp, nq), 0) * CMP_STRIDE + (CMP_BLOCK - 1)) <= t_lane
        cur = jnp.right_shift(t0 + lax.broadcasted_iota(jnp.int32, (SEL_ROWS, tq), 1),
                              int(math.log2(SLC_BLOCK)))
        bonus = jnp.where((j_sub == 0) | (j_sub == cur) | (j_sub == cur - 1), FORCE_BONUS, 0.0)
        valid = j_sub <= cur
        q_plain, q_slc, o_cmp = [], [], []
        for g in groups:
            heads = []
            for r in range(rep):
                hidx = g * rep + r
                c128 = qv[:, (hidx // 2) * LANE:(hidx // 2 + 1) * LANE]
                if hidx % 2:
                    c128 = pltpu.roll(c128, hd, 1)
                heads.append(jnp.where(lane_q < hd, c128 * (LOG2E * hd ** -0.5), 0.0))
            q_plain.append(jnp.concatenate(heads, axis=0).astype(BF16))

            kc = kcmp_ref[0]
            if g:
                kc = pltpu.roll(kc, hd, 1)
            kc = jnp.where(lane_c < hd, kc, 0.0).astype(BF16)
            s_t = jnp.where(cmask, _mm_nt(kc, q_plain[g]), NEG)
            e = jnp.exp2(s_t - jnp.max(s_t, axis=0, keepdims=True))
            p = jnp.where(cmask, e / jnp.sum(e, axis=0, keepdims=True), 0.0).astype(BF16)
            o_cmp.append(_mm(vc_t[g * hd:(g + 1) * hd].astype(BF16), p))
            imp4 = _mm(ovl_ref[...], p)
            imp = imp4[:, 0:tq]
            for r in range(1, rep):
                imp = imp + imp4[:, r * tq:(r + 1) * tq]

            score = jnp.where(valid, imp + bonus, NEG)
            grp = [score[SUBLANE * a:SUBLANE * (a + 1)] for a in range(ngrp)]
            rank = [jnp.zeros((SUBLANE, tq), F32) for _ in range(ngrp)]
            for k in range(SEL_ROWS):
                sk = jnp.broadcast_to(score[k:k + 1, :], (SUBLANE, tq))
                for a in range(ngrp):
                    if k < SUBLANE * a:
                        inc = jnp.where(sk >= grp[a], 1.0, 0.0)
                    elif k >= SUBLANE * (a + 1):
                        inc = jnp.where(sk > grp[a], 1.0, 0.0)
                    else:
                        inc = jnp.where(j8 > k - SUBLANE * a, jnp.where(sk >= grp[a], 1.0, 0.0),
                                        jnp.where(sk > grp[a], 1.0, 0.0))
                    rank[a] = rank[a] + inc
            bias_rows = [jnp.where(rk < ntop, 0.0, NEG) for rk in rank]
            b128 = jnp.concatenate([jnp.zeros((hd, tq), F32)] + bias_rows
                                   + [jnp.zeros((LANE - hd - SEL_ROWS, tq), F32)], axis=0)
            bq = b128.T
            q_slc.append(jnp.concatenate([h + bq for h in heads], axis=0).astype(BF16))
        return q_plain, q_slc, o_cmp

    prep = [prepare(qt) for qt in range(2)]
    q_plain = [p[0] for p in prep]
    q_slc = [p[1] for p in prep]

    chain = lambda br, qt, g: (br * 2 + qt) * NSA_KV_GROUPS + g
    m_ref[...] = jnp.full(m_ref.shape, NEG, F32)
    acc_ref[...] = jnp.zeros(acc_ref.shape, F32)
    tiled = lambda kind: jnp.concatenate([bias_ref[kind]] * rep, axis=1)

    def slc_step(t, carry):
        run([(chain(0, qt, g), g, 2 * t + u, q_slc[qt][g], None)
             for u in range(2) for qt in range(2) for g in groups])
        return carry

    lax.fori_loop(0, j, slc_step, 0)
    causal = tiled(BIAS_CAUSAL)
    run([(chain(0, 0, g), g, 2 * j, q_slc[0][g], causal) for g in groups]
        + [(chain(0, 1, g), g, 2 * j, q_slc[1][g], None) for g in groups]
        + [(chain(0, 1, g), g, 2 * j + 1, q_slc[1][g], causal) for g in groups])
    for qt in range(2):
        tiles = []
        for step, kind in enumerate((BIAS_WINLO, BIAS_ZERO, BIAS_CAUSAL)):
            kt = 2 * j + qt - 2 + step
            if step == 2:
                bias = causal
            elif step == 1 and qt == 1:
                bias = None
            else:
                bias = tiled(jnp.where(kt >= 0, kind, BIAS_NONE))
            tiles += [(chain(1, qt, g), 2 + g, jnp.maximum(kt, 0), q_plain[qt][g], bias) for g in groups]
        run(tiles)

    for qt in range(2):
        gates_t = jax.nn.sigmoid(gl_ref[0, qt * tq:(qt + 1) * tq, :]).T
        o_cmp = prep[qt][2]
        outs_t = []
        for g in groups:
            o_slc = finish(chain(0, qt, g))
            o_win = finish(chain(1, qt, g))
            for r in range(rep):
                c = (g * rep + r) * 3
                sl = slice(r * tq, (r + 1) * tq)
                outs_t.append(gates_t[c:c + 1] * o_cmp[g][:, sl] + gates_t[c + 1:c + 2] * o_slc[:, sl]
                              + gates_t[c + 2:c + 3] * o_win[:, sl])
        o_ref[0, qt * tq:(qt + 1) * tq, :] = _rms(jnp.concatenate(outs_t, axis=0).T, nw_ref[...]).astype(o_ref.dtype)


def _nsa(q3, kcmp, vcmp, kvs3, kvw3, gl3, consts, nw):
    b, s, _ = q3.shape
    tq = ATT_TQ
    ncp = kcmp.shape[1]
    row = lambda w: pl.BlockSpec((1, 2 * tq, w), lambda bi, i: (bi, i, 0))
    per_b = lambda n, w: pl.BlockSpec((1, n, w), lambda bi, i: (bi, 0, 0))
    return pl.pallas_call(
        functools.partial(_nsa_body, seq=s),
        grid=(b, s // (2 * tq)),
        in_specs=[row(NSA_WIDTH), per_b(ncp, KV_WIDTH), per_b(ncp, KV_WIDTH),
                  per_b(s, 2 * KV_WIDTH), per_b(s, 2 * KV_WIDTH), row(LANE)]
                 + [_full(c.shape) for c in consts] + [_full((1, NSA_WIDTH))],
        out_specs=row(NSA_WIDTH),
        out_shape=jax.ShapeDtypeStruct((b, s, NSA_WIDTH), BF16),
        scratch_shapes=[pltpu.VMEM((2 * NSA_KV_GROUPS, s, LANE), BF16),
                        pltpu.VMEM((2 * NSA_KV_GROUPS, V_ROWS, s), BF16),
                        pltpu.VMEM((4 * NSA_KV_GROUPS, V_ROWS, NSA_REP * tq), F32),
                        pltpu.VMEM((4 * NSA_KV_GROUPS, 1, NSA_REP * tq), F32)],
        compiler_params=_params("parallel", "arbitrary"),
        name="nsa_attn",
    )(q3, kcmp, vcmp, kvs3, kvw3, gl3, *consts, nw)


def _shifted_rows(x, halo, sh):
    r = pltpu.roll(x, sh, 0)
    row = lax.broadcasted_iota(jnp.int32, (HALO, x.shape[1]), 0)
    top = jnp.where(row < sh, pltpu.roll(halo, sh, 0), r[0:HALO])
    return jnp.concatenate([top, r[HALO:]], axis=0)


def _ssd_body(z_ref, xbc_ref, dt_ref, dtt_ref, cw_ref, cb_ref, dtb_row_ref, dtb_col_ref,
              alog_row_ref, alog_col_ref, d_ref, nw_ref, y_ref, state_ref, halo_ref):
    L = SSD_CHUNK
    hd, nst = SSD_HEAD_DIM, SSD_STATE
    rep = SSD_HEADS // SSD_GROUPS
    gw = rep * hd

    @pl.when(pl.program_id(1) == 0)
    def _():
        state_ref[...] = jnp.zeros_like(state_ref)
        halo_ref[...] = jnp.zeros_like(halo_ref)

    xr = xbc_ref[0]
    halo = halo_ref[...]
    conv = cb_ref[...] + cw_ref[SSD_CONV - 1:SSD_CONV, :] * xr
    for sh in range(1, SSD_CONV):
        conv = conv + cw_ref[SSD_CONV - 1 - sh:SSD_CONV - sh, :] * _shifted_rows(xr, halo, sh)
    halo_ref[...] = xr[L - HALO:L, :]
    xc = _silu(conv)
    xs = xc[:, :SSD_WIDTH]
    bm = xc[:, SSD_WIDTH:SSD_WIDTH + SSD_GROUPS * nst]
    cm = xc[:, SSD_WIDTH + SSD_GROUPS * nst:]

    dt_col = jax.nn.softplus(dt_ref[0] + dtb_row_ref[...])
    a_col = dt_col * (-jnp.exp(alog_row_ref[...]))
    i0 = lax.broadcasted_iota(jnp.int32, (L, L), 0)
    i1 = lax.broadcasted_iota(jnp.int32, (L, L), 1)
    tri = i0 >= i1
    acs_col = jnp.dot(jnp.where(tri, 1.0, 0.0), a_col, precision=lax.Precision.HIGHEST,
                      preferred_element_type=F32)
    dt_row = jax.nn.softplus(dtt_ref[0] + dtb_col_ref[...])
    a_row = dt_row * (-jnp.exp(alog_col_ref[...]))
    acs_row = jnp.dot(a_row, jnp.where(i0 <= i1, 1.0, 0.0), precision=lax.Precision.HIGHEST,
                      preferred_element_type=F32)
    acs_last = acs_col[L - 1:L, :]

    ys = []
    for g in range(SSD_GROUPS):
        bg = bm[:, g * nst:(g + 1) * nst].astype(BF16)
        cg = cm[:, g * nst:(g + 1) * nst].astype(BF16)
        cb = _mm_nt(cg, bg)
        state = state_ref[g]
        y_off = _mm(cg, state.astype(BF16))
        sx, cdec = [], []
        for r in range(rep):
            h = g * rep + r
            col = acs_col[:, h:h + 1]
            seg = col - acs_row[h:h + 1, :]
            xh = xs[:, h * hd:(h + 1) * hd]
            xdt = xh * dt_col[:, h:h + 1]
            y = _mm(jnp.where(tri, cb * jnp.exp(seg), 0.0).astype(BF16), xdt.astype(BF16))
            y = y + jnp.exp(col) * y_off[:, r * hd:(r + 1) * hd] + d_ref[:, h * hd:(h + 1) * hd] * xh
            ys.append(y)
            last = acs_last[:, h:h + 1]
            sx.append(xdt * jnp.exp(last - col))
            cdec.append(jnp.broadcast_to(jnp.exp(last), (1, hd)))
        sxg = jnp.concatenate(sx, axis=1).astype(BF16)
        new = lax.dot_general(bg, sxg, _TN, preferred_element_type=F32)
        state_ref[g] = state * jnp.concatenate(cdec, axis=1) + new
    y = jnp.concatenate(ys, axis=1) * _silu(z_ref[0])
    outs = []
    for g in range(SSD_GROUPS):
        outs.append(_rms(y[:, g * gw:(g + 1) * gw], nw_ref[:, g * gw:(g + 1) * gw]))
    y_ref[0] = jnp.concatenate(outs, axis=1).astype(y_ref.dtype)


def _ssd(z3, xbc3, dt3, dtt3, cw, cb, dtb_row, dtb_col, alog_row, alog_col, d_exp, nw):
    b, s, _ = z3.shape
    L = SSD_CHUNK
    row = lambda w: pl.BlockSpec((1, L, w), lambda bi, c: (bi, c, 0))
    small = [cw, cb, dtb_row, dtb_col, alog_row, alog_col, d_exp, nw]
    return pl.pallas_call(
        _ssd_body,
        grid=(b, s // L),
        in_specs=[row(SSD_WIDTH), row(SSD_XBC), row(LANE),
                  pl.BlockSpec((1, SSD_HEADS, L), lambda bi, c: (bi, 0, c))]
                 + [_full(a.shape) for a in small],
        out_specs=row(SSD_WIDTH),
        out_shape=jax.ShapeDtypeStruct((b, s, SSD_WIDTH), BF16),
        scratch_shapes=[pltpu.VMEM((SSD_GROUPS, SSD_STATE, SSD_WIDTH // SSD_GROUPS), F32),
                        pltpu.VMEM((HALO, SSD_XBC), F32)],
        compiler_params=_params("parallel", "arbitrary"),
        name="ssd",
    )(z3, xbc3, dt3, dtt3, *small)


def _ffn_body(x_ref, a_ref, s_ref, wa_ref, ws_ref, nw_ref, wg_ref, wu_ref, cw_ref, cb_ref, wd_ref, fw_ref,
              o_ref, halo_ref, act_ref, *, final):
    tm = x_ref.shape[1]
    nf = wg_ref.shape[1] // FFN_TF

    @pl.when(pl.program_id(1) == 0)
    def _():
        halo_ref[...] = jnp.zeros_like(halo_ref)

    x = x_ref[0] + _mm(a_ref[0], wa_ref[...]) + _mm(s_ref[0], ws_ref[...])
    h = _rms(x, nw_ref[...]).astype(BF16)
    for f in range(nf):
        gp = _mm(h, wg_ref[:, f * FFN_TF:(f + 1) * FFN_TF])
        up = _mm(h, wu_ref[:, f * FFN_TF:(f + 1) * FFN_TF])
        halo = halo_ref[f]
        cw = cw_ref[f]
        gate = cb_ref[f] + cw[FFN_CONV - 1:FFN_CONV, :] * gp
        for sh in range(1, FFN_CONV):
            gate = gate + cw[FFN_CONV - 1 - sh:FFN_CONV - sh, :] * _shifted_rows(gp, halo, sh)
        halo_ref[f] = gp[tm - HALO:tm, :]
        act_ref[:, f * FFN_TF:(f + 1) * FFN_TF] = (_silu(gate) * up).astype(BF16)
    y = x + _mm(act_ref[...], wd_ref[...])
    o_ref[0] = _rms(y, fw_ref[...]) if final else y


def _ffn(x3, a3, s3, wa, ws, nw, wg, wu, cw, cb, wd, fw, final, tm=FFN_TM):
    b, s, _ = x3.shape
    nf = wg.shape[1] // FFN_TF
    row = lambda w: pl.BlockSpec((1, tm, w), lambda bi, i: (bi, i, 0))
    once = lambda a: pl.BlockSpec(a.shape, lambda *_: (0,) * a.ndim, pipeline_mode=pl.Buffered(1))
    return pl.pallas_call(
        functools.partial(_ffn_body, final=final),
        grid=(b, s // tm),
        in_specs=[row(D_MODEL), row(NSA_WIDTH), row(SSD_WIDTH), once(wa), once(ws), _full(nw.shape),
                  once(wg), once(wu), _full(cw.shape), _full(cb.shape), once(wd), _full(fw.shape)],
        out_specs=row(D_MODEL),
        out_shape=jax.ShapeDtypeStruct((b, s, D_MODEL), F32),
        scratch_shapes=[pltpu.VMEM((nf, HALO, FFN_TF), F32), pltpu.VMEM((tm, D_FF), BF16)],
        compiler_params=_params("parallel", "arbitrary"),
        name="conv_ffn",
    )(x3, a3, s3, wa, ws, nw, wg, wu, cw, cb, wd, fw)


def _pad_lanes(v):
    return jnp.pad(v, (0, LANE - v.shape[0]))[None, :]


def kernel(x, norm_mix_w, w_in, cmp_pos_k, cmp_w1_k, cmp_b1_k, cmp_w2_k, cmp_pos_v, cmp_w1_v,
           cmp_b1_v, cmp_w2_v, nsa_norm_w, ssd_conv_w, ssd_conv_b, ssd_dt_bias, ssd_A_log, ssd_D,
           ssd_norm_w, w_out, norm_ffn_w, w_gate, w_up, ffn_conv_w, ffn_conv_b, w_down, norm_final_w):
    b, s, d = x.shape
    depth = w_in.shape[0]
    t = b * s
    nf = D_FF // FFN_TF
    nsa_consts = _nsa_consts(s)
    x3 = x
    for i in range(depth):
        x2 = x3.reshape(t, d)
        q, kc, vc, kvs, kvw, gl, dt, z, xbc = _in_proj(x2, norm_mix_w[i][None, :], _pack_w_in(w_in[i]))
        ncp = s // CMP_STRIDE
        kcmp, vcmp = _compress(
            kc.reshape(b, ncp, CMP_STRIDE * KV_WIDTH), vc.reshape(b, ncp, CMP_STRIDE * KV_WIDTH),
            _pack_compress(cmp_pos_k[i], cmp_w1_k[i], cmp_b1_k[i], cmp_w2_k[i]),
            _pack_compress(cmp_pos_v[i], cmp_w1_v[i], cmp_b1_v[i], cmp_w2_v[i]))
        o_attn = _nsa(q.reshape(b, s, -1), kcmp, vcmp, kvs.reshape(b, s, -1), kvw.reshape(b, s, -1),
                      gl.reshape(b, s, -1), nsa_consts, nsa_norm_w[i][None, :])
        dt3 = dt.reshape(b, s, LANE)
        dtt3 = jnp.swapaxes(dt3[:, :, :SSD_HEADS], 1, 2)
        o_ssd = _ssd(z.reshape(b, s, -1), xbc.reshape(b, s, -1), dt3, dtt3,
                     ssd_conv_w[i], ssd_conv_b[i][None, :],
                     _pad_lanes(ssd_dt_bias[i]), ssd_dt_bias[i][:, None],
                     _pad_lanes(ssd_A_log[i]), ssd_A_log[i][:, None],
                     jnp.repeat(ssd_D[i], SSD_HEAD_DIM)[None, :], ssd_norm_w[i][None, :])
        wo = w_out[i].astype(BF16)
        x3 = _ffn(x3, o_attn, o_ssd, wo[:NSA_WIDTH], wo[NSA_WIDTH:], norm_ffn_w[i][None, :],
                  w_gate[i].astype(BF16), w_up[i].astype(BF16),
                  ffn_conv_w[i].reshape(FFN_CONV, nf, FFN_TF).transpose(1, 0, 2),
                  ffn_conv_b[i].reshape(nf, 1, FFN_TF),
                  w_down[i].astype(BF16), norm_final_w[None, :], final=(i == depth - 1))
    return x3
```

```python
import functools
import math

import numpy as np
import jax
import jax.numpy as jnp
from jax import lax
from jax.experimental import pallas as pl
from jax.experimental.pallas import tpu as pltpu

F32 = jnp.float32
BF16 = jnp.bfloat16

D_MODEL = 1024
NSA_HEADS = 8
NSA_KV_GROUPS = 2
NSA_HEAD_DIM = 64
NSA_REP = NSA_HEADS // NSA_KV_GROUPS
NSA_WIDTH = NSA_HEADS * NSA_HEAD_DIM
KV_WIDTH = NSA_KV_GROUPS * NSA_HEAD_DIM
CMP_BLOCK = 32
CMP_STRIDE = 16
CMP_HIDDEN = 128
SLC_BLOCK = 64
SLC_TOP = 16
WIN = 512
SSD_HEADS = 8
SSD_HEAD_DIM = 64
SSD_WIDTH = SSD_HEADS * SSD_HEAD_DIM
SSD_GROUPS = 2
SSD_STATE = 128
SSD_CONV = 4
SSD_CHUNK = 256
SSD_XBC = SSD_WIDTH + 2 * SSD_GROUPS * SSD_STATE
D_FF = 2816
FFN_CONV = 3
RMS_EPS = 1e-6
NEG = -1e30
LOG2E = math.log2(math.e)
FORCE_BONUS = 1e4

IN_SIZES = [NSA_WIDTH, KV_WIDTH, KV_WIDTH, KV_WIDTH, KV_WIDTH, KV_WIDTH, KV_WIDTH,
            3 * NSA_HEADS, SSD_WIDTH, SSD_XBC, SSD_HEADS]
IN_SPLITS = [int(v) for v in np.cumsum(IN_SIZES)[:-1]]

LANE = 128
SUBLANE = 8
HALO = SUBLANE
ATT_TQ = 256
ATT_TK = 256
FFN_TF = 256
FFN_TM = 1024
VMEM_LIMIT = 56 * 1024 * 1024

_NT = (((1,), (1,)), ((), ()))
_TN = (((0,), (0,)), ((), ()))


def _mm(a, b):
    return jnp.dot(a, b, preferred_element_type=F32)


def _mm_nt(a, b):
    return lax.dot_general(a, b, _NT, preferred_element_type=F32)


def _rms(x, w):
    return x * lax.rsqrt(jnp.mean(x * x, axis=-1, keepdims=True) + RMS_EPS) * w


def _silu(x):
    return x * jax.nn.sigmoid(x)


def _params(*sem):
    return pltpu.CompilerParams(dimension_semantics=sem, vmem_limit_bytes=VMEM_LIMIT)


def _full(shape):
    n = len(shape)
    return pl.BlockSpec(shape, lambda *_: (0,) * n)


_PROJ_WIDTHS = (NSA_WIDTH, KV_WIDTH, KV_WIDTH, 2 * KV_WIDTH, 2 * KV_WIDTH, LANE, LANE,
                SSD_WIDTH, SSD_XBC)


def _pack_w_in(w):
    q, kc, vc, ks, vs, kw, vw, gl, z, xbc, dt = jnp.split(w, IN_SPLITS, axis=-1)
    pad = lambda a: jnp.pad(a, ((0, 0), (0, LANE - a.shape[1])))
    return jnp.concatenate([q, kc, vc, ks, vs, kw, vw, pad(gl), pad(dt), z, xbc], axis=-1).astype(BF16)


def _in_proj_body(x_ref, nw_ref, w_ref, *out_refs):
    h = _rms(x_ref[...], nw_ref[...]).astype(BF16)
    off = 0
    pending = list(out_refs)
    while pending:
        n_out = 2 if (pending[0].shape[-1] == LANE and len(pending) > 1 and pending[1].shape[-1] == LANE) else 1
        refs, pending = pending[:n_out], pending[n_out:]
        n = sum(r.shape[-1] for r in refs)
        res = _mm(h, w_ref[:, off:off + n])
        col = 0
        for r in refs:
            r[...] = res[:, col:col + r.shape[-1]]
            col += r.shape[-1]
        off += n


def _in_proj(x2, nw, w_packed, tm=512):
    t = x2.shape[0]
    ncol = w_packed.shape[1]
    return pl.pallas_call(
        _in_proj_body,
        grid=(t // tm,),
        in_specs=[pl.BlockSpec((tm, D_MODEL), lambda i: (i, 0)),
                  _full((1, D_MODEL)), _full((D_MODEL, ncol))],
        out_specs=[pl.BlockSpec((tm, n), lambda i: (i, 0)) for n in _PROJ_WIDTHS],
        out_shape=[jax.ShapeDtypeStruct((t, n), F32) for n in _PROJ_WIDTHS],
        compiler_params=_params("parallel"),
        name="in_proj",
    )(x2, nw, w_packed)


def _pack_compress(pos, w1, b1, w2):
    half = CMP_BLOCK // 2
    g = NSA_KV_GROUPS
    eye = jnp.eye(g, dtype=F32)
    w1r = w1.reshape(CMP_BLOCK, NSA_HEAD_DIM, CMP_HIDDEN)
    w1e = jnp.einsum('ldn,gh->lgdhn', w1r, eye).reshape(2, half * g * NSA_HEAD_DIM, g * CMP_HIDDEN)
    pose = jnp.broadcast_to(pos[:, None, :], (CMP_BLOCK, g, NSA_HEAD_DIM)).reshape(2, half * g * NSA_HEAD_DIM)
    b1e = jnp.tile(b1, g)[None, :]
    w2e = jnp.einsum('nd,gh->gnhd', w2, eye).reshape(g * CMP_HIDDEN, g * NSA_HEAD_DIM)
    return pose, w1e.astype(BF16), b1e, w2e.astype(BF16)


def _compress_body(kc_ref, vc_ref, pk_ref, wk1_ref, bk1_ref, wk2_ref,
                   pv_ref, wv1_ref, bv1_ref, wv2_ref, ko_ref, vo_ref):
    def one(r_ref, pos_ref, w1_ref, b1_ref, w2_ref, o_ref):
        r = r_ref[0]
        ncp = r.shape[0]
        lo = _mm((r + pos_ref[0:1, :]).astype(BF16), w1_ref[0])
        hi = _mm((r + pos_ref[1:2, :]).astype(BF16), w1_ref[1])
        hid = lo + pltpu.roll(hi, ncp - 1, 0) + b1_ref[...]
        hid = jax.nn.gelu(hid)
        o_ref[0] = _mm(hid.astype(BF16), w2_ref[...])

    one(kc_ref, pk_ref, wk1_ref, bk1_ref, wk2_ref, ko_ref)
    one(vc_ref, pv_ref, wv1_ref, bv1_ref, wv2_ref, vo_ref)


def _compress(kc3, vc3, pk, pv):
    b, ncp, wide = kc3.shape
    blk = pl.BlockSpec((1, ncp, wide), lambda i: (i, 0, 0))
    oblk = pl.BlockSpec((1, ncp, KV_WIDTH), lambda i: (i, 0, 0))
    wspecs = lambda p: [_full(a.shape) for a in p]
    return pl.pallas_call(
        _compress_body,
        grid=(b,),
        in_specs=[blk, blk] + wspecs(pk) + wspecs(pv),
        out_specs=[oblk, oblk],
        out_shape=[jax.ShapeDtypeStruct((b, ncp, KV_WIDTH), F32)] * 2,
        compiler_params=_params("parallel"),
        name="compress",
    )(kc3, vc3, *pk, *pv)


BIAS_ZERO, BIAS_CAUSAL, BIAS_WINLO, BIAS_NONE = range(4)
SEL_ROWS = 32
V_ROWS = 80
assert ATT_TQ == ATT_TK and WIN == 2 * ATT_TK and KV_WIDTH == LANE and 2 * NSA_HEAD_DIM == LANE


def _nsa_consts(s):
    ncp = s // CMP_STRIDE
    ns = s // SLC_BLOCK
    assert ns <= SEL_ROWS
    c0 = np.arange(ncp)[None, :] * CMP_STRIDE
    jj = np.arange(SEL_ROWS)[:, None]
    ovl_t = (c0 < jj * SLC_BLOCK + SLC_BLOCK) & (c0 + CMP_BLOCK > jj * SLC_BLOCK) & (jj < ns)
    onehot = np.zeros((s, LANE), np.float32)
    onehot[np.arange(s), NSA_HEAD_DIM + np.arange(s) // SLC_BLOCK] = 1.0
    k = np.arange(ATT_TK)[:, None]
    t = np.arange(ATT_TQ)[None, :]
    bias = np.stack([np.zeros((ATT_TK, ATT_TQ)),
                     np.where(k <= t, 0.0, NEG),
                     np.where(k > t, 0.0, NEG),
                     np.full((ATT_TK, ATT_TQ), NEG)]).astype(np.float32)
    return jnp.asarray(ovl_t.astype(np.float32), BF16), jnp.asarray(onehot), jnp.asarray(bias)


def _nsa_body(q_ref, kcmp_ref, vcmp_ref, kvs_ref, kvw_ref, gl_ref, ovl_ref, onehot_ref, bias_ref,
              nw_ref, o_ref, kaug_ref, vaug_ref, acc_ref, m_ref, s_ref, *, seq):
    tq, tk, hd, rep = ATT_TQ, ATT_TK, NSA_HEAD_DIM, NSA_REP
    j = pl.program_id(1)
    ntop = min(SLC_TOP, seq // SLC_BLOCK)
    ncp = kcmp_ref.shape[1]
    nq = rep * tq

    @pl.when(j == 0)
    def _():
        lane = lax.broadcasted_iota(jnp.int32, (tk, LANE), 1)
        ones_col = jnp.where(lane == hd, 1.0, 0.0)
        for br, kv_ref in enumerate((kvs_ref, kvw_ref)):
            def chunk(c, carry, br=br, kv_ref=kv_ref):
                r0 = pl.multiple_of(c * tk, tk)
                k128 = kv_ref[0, pl.ds(r0, tk), 0:KV_WIDTH]
                v128 = kv_ref[0, pl.ds(r0, tk), KV_WIDTH:2 * KV_WIDTH]
                extra = onehot_ref[pl.ds(r0, tk), :] if br == 0 else 0.0
                for g in range(NSA_KV_GROUPS):
                    kk = k128 if g == 0 else pltpu.roll(k128, hd, 1)
                    vv = v128 if g == 0 else pltpu.roll(v128, hd, 1)
                    kaug_ref[2 * br + g, pl.ds(r0, tk), :] = jnp.where(lane < hd, kk, extra).astype(BF16)
                    vt = jnp.where(lane < hd, vv, ones_col).T
                    vaug_ref[2 * br + g, :, pl.ds(r0, tk)] = vt[0:V_ROWS].astype(BF16)
                return carry
            lax.fori_loop(0, seq // tk, chunk, 0)

    def scores(kv_slot, kt, q_all):
        k0 = pl.multiple_of(kt * tk, tk)
        return _mm_nt(kaug_ref[kv_slot, pl.ds(k0, tk), :], q_all)

    def absorb(items):
        stats = []
        for chain, _, _, n in items:
            s_t = s_ref[n]
            m_old = m_ref[chain]
            m_new = jnp.maximum(m_old, jnp.max(s_t, axis=0, keepdims=True))
            stats.append((m_old, m_new, jnp.exp2(s_t - m_new).astype(BF16)))
        pvs = [_mm(vaug_ref[kv_slot, :, pl.ds(pl.multiple_of(kt * tk, tk), tk)], p)
               for (_, kv_slot, kt, _), (_, _, p) in zip(items, stats)]
        for (chain, _, _, _), (m_old, m_new, _), pv in zip(items, stats, pvs):
            acc_ref[chain] = jnp.exp2(m_old - m_new) * acc_ref[chain] + pv
            m_ref[chain] = m_new

    def run(tiles):
        for n, (chain, kv_slot, kt, q, bias) in enumerate(tiles):
            s_ref[n] = scores(kv_slot, kt, q) if bias is None else scores(kv_slot, kt, q) + bias
        items = [(chain, kv_slot, kt, n) for n, (chain, kv_slot, kt, q, bias) in enumerate(tiles)]
        for n in range(0, len(items), 2):
            absorb(items[n:n + 2])

    def finish(chain):
        acc = acc_ref[chain]
        return acc[0:hd] / acc[hd:hd + 1]

    lane_q = lax.broadcasted_iota(jnp.int32, (tq, LANE), 1)
    lane_c = lax.broadcasted_iota(jnp.int32, (ncp, LANE), 1)
    vc_t = vcmp_ref[0].T
    j_sub = lax.broadcasted_iota(jnp.int32, (SEL_ROWS, tq), 0)
    j8 = lax.broadcasted_iota(jnp.int32, (SUBLANE, tq), 0)
    ngrp = SEL_ROWS // SUBLANE
    groups = range(NSA_KV_GROUPS)

    def prepare(qt):
        t0 = (2 * j + qt) * tq
        qv = q_ref[0, qt * tq:(qt + 1) * tq, :]
        t_lane = t0 + (lax.broadcasted_iota(jnp.int32, (ncp, nq), 1) & (tq - 1))
        cmask = (lax.broadcasted_iota(jnp.int32, (ncp, nq), 0) * CMP_STRIDE + (CMP_BLOCK - 1)) <= t_lane
        cur = jnp.right_shift(t0 + lax.broadcasted_iota(jnp.int32, (SEL_ROWS, tq), 1),
                              int(math.log2(SLC_BLOCK)))
        bonus = jnp.where((j_sub == 0) | (j_sub == cur) | (j_sub == cur - 1), FORCE_BONUS, 0.0)
        valid = j_sub <= cur
        q_plain, q_slc, o_cmp = [], [], []
        for g in groups:
            heads = []
            for r in range(rep):
                hidx = g * rep + r
                c128 = qv[:, (hidx // 2) * LANE:(hidx // 2 + 1) * LANE]
                if hidx % 2:
                    c128 = pltpu.roll(c128, hd, 1)
                heads.append(jnp.where(lane_q < hd, c128 * (LOG2E * hd ** -0.5), 0.0))
            q_plain.append(jnp.concatenate(heads, axis=0).astype(BF16))

            kc = kcmp_ref[0]
            if g:
                kc = pltpu.roll(kc, hd, 1)
            kc = jnp.where(lane_c < hd, kc, 0.0).astype(BF16)
            s_t = jnp.where(cmask, _mm_nt(kc, q_plain[g]), NEG)
            e = jnp.exp2(s_t - jnp.max(s_t, axis=0, keepdims=True))
            p = jnp.where(cmask, e / jnp.sum(e, axis=0, keepdims=True), 0.0).astype(BF16)
            o_cmp.append(_mm(vc_t[g * hd:(g + 1) * hd].astype(BF16), p))
            imp4 = _mm(ovl_ref[...], p)
            imp = imp4[:, 0:tq]
            for r in range(1, rep):
                imp = imp + imp4[:, r * tq:(r + 1) * tq]

            score = jnp.where(valid, imp + bonus, NEG)
            grp = [score[SUBLANE * a:SUBLANE * (a + 1)] for a in range(ngrp)]
            rank = [jnp.zeros((SUBLANE, tq), F32) for _ in range(ngrp)]
            for k in range(SEL_ROWS):
                sk = jnp.broadcast_to(score[k:k + 1, :], (SUBLANE, tq))
                for a in range(ngrp):
                    if k < SUBLANE * a:
                        inc = jnp.where(sk >= grp[a], 1.0, 0.0)
                    elif k >= SUBLANE * (a + 1):
                        inc = jnp.where(sk > grp[a], 1.0, 0.0)
                    else:
                        inc = jnp.where(j8 > k - SUBLANE * a, jnp.where(sk >= grp[a], 1.0, 0.0),
                                        jnp.where(sk > grp[a], 1.0, 0.0))
                    rank[a] = rank[a] + inc
            bias_rows = [jnp.where(rk < ntop, 0.0, NEG) for rk in rank]
            b128 = jnp.concatenate([jnp.zeros((hd, tq), F32)] + bias_rows
                                   + [jnp.zeros((LANE - hd - SEL_ROWS, tq), F32)], axis=0)
            bq = b128.T
            q_slc.append(jnp.concatenate([h + bq for h in heads], axis=0).astype(BF16))
        return q_plain, q_slc, o_cmp

    prep = [prepare(qt) for qt in range(2)]
    q_plain = [p[0] for p in prep]
    q_slc = [p[1] for p in prep]

    chain = lambda br, qt, g: (br * 2 + qt) * NSA_KV_GROUPS + g
    m_ref[...] = jnp.full(m_ref.shape, NEG, F32)
    acc_ref[...] = jnp.zeros(acc_ref.shape, F32)
    tiled = lambda kind: jnp.concatenate([bias_ref[kind]] * rep, axis=1)

    def slc_step(t, carry):
        run([(chain(0, qt, g), g, 2 * t + u, q_slc[qt][g], None)
             for u in range(2) for qt in range(2) for g in groups])
        return carry

    lax.fori_loop(0, j, slc_step, 0)
    causal = tiled(BIAS_CAUSAL)
    run([(chain(0, 0, g), g, 2 * j, q_slc[0][g], causal) for g in groups]
        + [(chain(0, 1, g), g, 2 * j, q_slc[1][g], None) for g in groups]
        + [(chain(0, 1, g), g, 2 * j + 1, q_slc[1][g], causal) for g in groups])
    for qt in range(2):
        tiles = []
        for step, kind in enumerate((BIAS_WINLO, BIAS_ZERO, BIAS_CAUSAL)):
            kt = 2 * j + qt - 2 + step
            if step == 2:
                bias = causal
            elif step == 1 and qt == 1:
                bias = None
            else:
                bias = tiled(jnp.where(kt >= 0, kind, BIAS_NONE))
            tiles += [(chain(1, qt, g), 2 + g, jnp.maximum(kt, 0), q_plain[qt][g], bias) for g in groups]
        run(tiles)

    for qt in range(2):
        gates_t = jax.nn.sigmoid(gl_ref[0, qt * tq:(qt + 1) * tq, :]).T
        o_cmp = prep[qt][2]
        outs_t = []
        for g in groups:
            o_slc = finish(chain(0, qt, g))
            o_win = finish(chain(1, qt, g))
            for r in range(rep):
                c = (g * rep + r) * 3
                sl = slice(r * tq, (r + 1) * tq)
                outs_t.append(gates_t[c:c + 1] * o_cmp[g][:, sl] + gates_t[c + 1:c + 2] * o_slc[:, sl]
                              + gates_t[c + 2:c + 3] * o_win[:, sl])
        o_ref[0, qt * tq:(qt + 1) * tq, :] = _rms(jnp.concatenate(outs_t, axis=0).T, nw_ref[...]).astype(o_ref.dtype)


def _nsa(q3, kcmp, vcmp, kvs3, kvw3, gl3, consts, nw):
    b, s, _ = q3.shape
    tq = ATT_TQ
    ncp = kcmp.shape[1]
    row = lambda w: pl.BlockSpec((1, 2 * tq, w), lambda bi, i: (bi, i, 0))
    per_b = lambda n, w: pl.BlockSpec((1, n, w), lambda bi, i: (bi, 0, 0))
    return pl.pallas_call(
        functools.partial(_nsa_body, seq=s),
        grid=(b, s // (2 * tq)),
        in_specs=[row(NSA_WIDTH), per_b(ncp, KV_WIDTH), per_b(ncp, KV_WIDTH),
                  per_b(s, 2 * KV_WIDTH), per_b(s, 2 * KV_WIDTH), row(LANE)]
                 + [_full(c.shape) for c in consts] + [_full((1, NSA_WIDTH))],
        out_specs=row(NSA_WIDTH),
        out_shape=jax.ShapeDtypeStruct((b, s, NSA_WIDTH), BF16),
        scratch_shapes=[pltpu.VMEM((2 * NSA_KV_GROUPS, s, LANE), BF16),
                        pltpu.VMEM((2 * NSA_KV_GROUPS, V_ROWS, s), BF16),
                        pltpu.VMEM((4 * NSA_KV_GROUPS, V_ROWS, NSA_REP * tq), F32),
                        pltpu.VMEM((4 * NSA_KV_GROUPS, 1, NSA_REP * tq), F32),
                        pltpu.VMEM((4 * NSA_KV_GROUPS, ATT_TK, NSA_REP * tq), F32)],
        compiler_params=_params("parallel", "arbitrary"),
        name="nsa_attn",
    )(q3, kcmp, vcmp, kvs3, kvw3, gl3, *consts, nw)


def _shifted_rows(x, halo, sh):
    r = pltpu.roll(x, sh, 0)
    row = lax.broadcasted_iota(jnp.int32, (HALO, x.shape[1]), 0)
    top = jnp.where(row < sh, pltpu.roll(halo, sh, 0), r[0:HALO])
    return jnp.concatenate([top, r[HALO:]], axis=0)


def _ssd_body(z_ref, xbc_ref, dt_ref, dtt_ref, cw_ref, cb_ref, dtb_row_ref, dtb_col_ref,
              alog_row_ref, alog_col_ref, d_ref, nw_ref, y_ref, state_ref, halo_ref):
    L = SSD_CHUNK
    hd, nst = SSD_HEAD_DIM, SSD_STATE
    rep = SSD_HEADS // SSD_GROUPS
    gw = rep * hd

    @pl.when(pl.program_id(1) == 0)
    def _():
        state_ref[...] = jnp.zeros_like(state_ref)
        halo_ref[...] = jnp.zeros_like(halo_ref)

    xr = xbc_ref[0]
    halo = halo_ref[...]
    conv = cb_ref[...] + cw_ref[SSD_CONV - 1:SSD_CONV, :] * xr
    for sh in range(1, SSD_CONV):
        conv = conv + cw_ref[SSD_CONV - 1 - sh:SSD_CONV - sh, :] * _shifted_rows(xr, halo, sh)
    halo_ref[...] = xr[L - HALO:L, :]
    xc = _silu(conv)
    xs = xc[:, :SSD_WIDTH]
    bm = xc[:, SSD_WIDTH:SSD_WIDTH + SSD_GROUPS * nst]
    cm = xc[:, SSD_WIDTH + SSD_GROUPS * nst:]

    dt_col = jax.nn.softplus(dt_ref[0] + dtb_row_ref[...])
    a_col = dt_col * (-jnp.exp(alog_row_ref[...]))
    i0 = lax.broadcasted_iota(jnp.int32, (L, L), 0)
    i1 = lax.broadcasted_iota(jnp.int32, (L, L), 1)
    tri = i0 >= i1
    acs_col = jnp.dot(jnp.where(tri, 1.0, 0.0), a_col, precision=lax.Precision.HIGHEST,
                      preferred_element_type=F32)
    dt_row = jax.nn.softplus(dtt_ref[0] + dtb_col_ref[...])
    a_row = dt_row * (-jnp.exp(alog_col_ref[...]))
    acs_row = jnp.dot(a_row, jnp.where(i0 <= i1, 1.0, 0.0), precision=lax.Precision.HIGHEST,
                      preferred_element_type=F32)
    acs_last = acs_col[L - 1:L, :]

    ys = []
    for g in range(SSD_GROUPS):
        bg = bm[:, g * nst:(g + 1) * nst].astype(BF16)
        cg = cm[:, g * nst:(g + 1) * nst].astype(BF16)
        cb = _mm_nt(cg, bg)
        state = state_ref[g]
        y_off = _mm(cg, state.astype(BF16))
        sx, cdec = [], []
        for r in range(rep):
            h = g * rep + r
            col = acs_col[:, h:h + 1]
            seg = col - acs_row[h:h + 1, :]
            xh = xs[:, h * hd:(h + 1) * hd]
            xdt = xh * dt_col[:, h:h + 1]
            y = _mm(jnp.where(tri, cb * jnp.exp(seg), 0.0).astype(BF16), xdt.astype(BF16))
            y = y + jnp.exp(col) * y_off[:, r * hd:(r + 1) * hd] + d_ref[:, h * hd:(h + 1) * hd] * xh
            ys.append(y)
            last = acs_last[:, h:h + 1]
            sx.append(xdt * jnp.exp(last - col))
            cdec.append(jnp.broadcast_to(jnp.exp(last), (1, hd)))
        sxg = jnp.concatenate(sx, axis=1).astype(BF16)
        new = lax.dot_general(bg, sxg, _TN, preferred_element_type=F32)
        state_ref[g] = state * jnp.concatenate(cdec, axis=1) + new
    y = jnp.concatenate(ys, axis=1) * _silu(z_ref[0])
    outs = []
    for g in range(SSD_GROUPS):
        outs.append(_rms(y[:, g * gw:(g + 1) * gw], nw_ref[:, g * gw:(g + 1) * gw]))
    y_ref[0] = jnp.concatenate(outs, axis=1).astype(y_ref.dtype)


def _ssd(z3, xbc3, dt3, dtt3, cw, cb, dtb_row, dtb_col, alog_row, alog_col, d_exp, nw):
    b, s, _ = z3.shape
    L = SSD_CHUNK
    row = lambda w: pl.BlockSpec((1, L, w), lambda bi, c: (bi, c, 0))
    small = [cw, cb, dtb_row, dtb_col, alog_row, alog_col, d_exp, nw]
    return pl.pallas_call(
        _ssd_body,
        grid=(b, s // L),
        in_specs=[row(SSD_WIDTH), row(SSD_XBC), row(LANE),
                  pl.BlockSpec((1, SSD_HEADS, L), lambda bi, c: (bi, 0, c))]
                 + [_full(a.shape) for a in small],
        out_specs=row(SSD_WIDTH),
        out_shape=jax.ShapeDtypeStruct((b, s, SSD_WIDTH), BF16),
        scratch_shapes=[pltpu.VMEM((SSD_GROUPS, SSD_STATE, SSD_WIDTH // SSD_GROUPS), F32),
                        pltpu.VMEM((HALO, SSD_XBC), F32)],
        compiler_params=_params("parallel", "arbitrary"),
        name="ssd",
    )(z3, xbc3, dt3, dtt3, *small)


def _ffn_body(x_ref, a_ref, s_ref, wa_ref, ws_ref, nw_ref, wg_ref, wu_ref, cw_ref, cb_ref, wd_ref, fw_ref,
              o_ref, halo_ref, act_ref, *, final):
    tm = x_ref.shape[1]
    nf = wg_ref.shape[1] // FFN_TF

    @pl.when(pl.program_id(1) == 0)
    def _():
        halo_ref[...] = jnp.zeros_like(halo_ref)

    x = x_ref[0] + _mm(a_ref[0], wa_ref[...]) + _mm(s_ref[0], ws_ref[...])
    h = _rms(x, nw_ref[...]).astype(BF16)
    for f in range(nf):
        gp = _mm(h, wg_ref[:, f * FFN_TF:(f + 1) * FFN_TF])
        up = _mm(h, wu_ref[:, f * FFN_TF:(f + 1) * FFN_TF])
        halo = halo_ref[f]
        cw = cw_ref[f]
        gate = cb_ref[f] + cw[FFN_CONV - 1:FFN_CONV, :] * gp
        for sh in range(1, FFN_CONV):
            gate = gate + cw[FFN_CONV - 1 - sh:FFN_CONV - sh, :] * _shifted_rows(gp, halo, sh)
        halo_ref[f] = gp[tm - HALO:tm, :]
        act_ref[:, f * FFN_TF:(f + 1) * FFN_TF] = (_silu(gate) * up).astype(BF16)
    y = x + _mm(act_ref[...], wd_ref[...])
    o_ref[0] = _rms(y, fw_ref[...]) if final else y


def _ffn(x3, a3, s3, wa, ws, nw, wg, wu, cw, cb, wd, fw, final, tm=FFN_TM):
    b, s, _ = x3.shape
    nf = wg.shape[1] // FFN_TF
    row = lambda w: pl.BlockSpec((1, tm, w), lambda bi, i: (bi, i, 0))
    once = lambda a: pl.BlockSpec(a.shape, lambda *_: (0,) * a.ndim, pipeline_mode=pl.Buffered(1))
    return pl.pallas_call(
        functools.partial(_ffn_body, final=final),
        grid=(b, s // tm),
        in_specs=[row(D_MODEL), row(NSA_WIDTH), row(SSD_WIDTH), once(wa), once(ws), _full(nw.shape),
                  once(wg), once(wu), _full(cw.shape), _full(cb.shape), once(wd), _full(fw.shape)],
        out_specs=row(D_MODEL),
        out_shape=jax.ShapeDtypeStruct((b, s, D_MODEL), F32),
        scratch_shapes=[pltpu.VMEM((nf, HALO, FFN_TF), F32), pltpu.VMEM((tm, D_FF), BF16)],
        compiler_params=_params("parallel", "arbitrary"),
        name="conv_ffn",
    )(x3, a3, s3, wa, ws, nw, wg, wu, cw, cb, wd, fw)


def _pad_lanes(v):
    return jnp.pad(v, (0, LANE - v.shape[0]))[None, :]


def kernel(x, norm_mix_w, w_in, cmp_pos_k, cmp_w1_k, cmp_b1_k, cmp_w2_k, cmp_pos_v, cmp_w1_v,
           cmp_b1_v, cmp_w2_v, nsa_norm_w, ssd_conv_w, ssd_conv_b, ssd_dt_bias, ssd_A_log, ssd_D,
           ssd_norm_w, w_out, norm_ffn_w, w_gate, w_up, ffn_conv_w, ffn_conv_b, w_down, norm_final_w):
    b, s, d = x.shape
    depth = w_in.shape[0]
    t = b * s
    nf = D_FF // FFN_TF
    nsa_consts = _nsa_consts(s)
    x3 = x
    for i in range(depth):
        x2 = x3.reshape(t, d)
        q, kc, vc, kvs, kvw, gl, dt, z, xbc = _in_proj(x2, norm_mix_w[i][None, :], _pack_w_in(w_in[i]))
        ncp = s // CMP_STRIDE
        kcmp, vcmp = _compress(
            kc.reshape(b, ncp, CMP_STRIDE * KV_WIDTH), vc.reshape(b, ncp, CMP_STRIDE * KV_WIDTH),
            _pack_compress(cmp_pos_k[i], cmp_w1_k[i], cmp_b1_k[i], cmp_w2_k[i]),
            _pack_compress(cmp_pos_v[i], cmp_w1_v[i], cmp_b1_v[i], cmp_w2_v[i]))
        o_attn = _nsa(q.reshape(b, s, -1), kcmp, vcmp, kvs.reshape(b, s, -1), kvw.reshape(b, s, -1),
                      gl.reshape(b, s, -1), nsa_consts, nsa_norm_w[i][None, :])
        dt3 = dt.reshape(b, s, LANE)
        dtt3 = jnp.swapaxes(dt3[:, :, :SSD_HEADS], 1, 2)
        o_ssd = _ssd(z.reshape(b, s, -1), xbc.reshape(b, s, -1), dt3, dtt3,
                     ssd_conv_w[i], ssd_conv_b[i][None, :],
                     _pad_lanes(ssd_dt_bias[i]), ssd_dt_bias[i][:, None],
                     _pad_lanes(ssd_A_log[i]), ssd_A_log[i][:, None],
                     jnp.repeat(ssd_D[i], SSD_HEAD_DIM)[None, :], ssd_norm_w[i][None, :])
        wo = w_out[i].astype(BF16)
        x3 = _ffn(x3, o_attn, o_ssd, wo[:NSA_WIDTH], wo[NSA_WIDTH:], norm_ffn_w[i][None, :],
                  w_gate[i].astype(BF16), w_up[i].astype(BF16),
                  ffn_conv_w[i].reshape(FFN_CONV, nf, FFN_TF).transpose(1, 0, 2),
                  ffn_conv_b[i].reshape(nf, 1, FFN_TF),
                  w_down[i].astype(BF16), norm_final_w[None, :], final=(i == depth - 1))
    return x3
```

```python
import functools
import math

import numpy as np
import jax
import jax.numpy as jnp
from jax import lax
from jax.experimental import pallas as pl
from jax.experimental.pallas import tpu as pltpu

F32 = jnp.float32
BF16 = jnp.bfloat16

D_MODEL = 1024
NSA_HEADS = 8
NSA_KV_GROUPS = 2
NSA_HEAD_DIM = 64
NSA_REP = NSA_HEADS // NSA_KV_GROUPS
NSA_WIDTH = NSA_HEADS * NSA_HEAD_DIM
KV_WIDTH = NSA_KV_GROUPS * NSA_HEAD_DIM
CMP_BLOCK = 32
CMP_STRIDE = 16
CMP_HIDDEN = 128
SLC_BLOCK = 64
SLC_TOP = 16
WIN = 512
SSD_HEADS = 8
SSD_HEAD_DIM = 64
SSD_WIDTH = SSD_HEADS * SSD_HEAD_DIM
SSD_GROUPS = 2
SSD_STATE = 128
SSD_CONV = 4
SSD_CHUNK = 256
SSD_XBC = SSD_WIDTH + 2 * SSD_GROUPS * SSD_STATE
D_FF = 2816
FFN_CONV = 3
RMS_EPS = 1e-6
NEG = -1e30
LOG2E = math.log2(math.e)
FORCE_BONUS = 1e4

IN_SIZES = [NSA_WIDTH, KV_WIDTH, KV_WIDTH, KV_WIDTH, KV_WIDTH, KV_WIDTH, KV_WIDTH,
            3 * NSA_HEADS, SSD_WIDTH, SSD_XBC, SSD_HEADS]
IN_SPLITS = [int(v) for v in np.cumsum(IN_SIZES)[:-1]]

LANE = 128
SUBLANE = 8
HALO = SUBLANE
ATT_TQ = 256
ATT_TK = 256
FFN_TF = 256
FFN_TM = 1024
VMEM_LIMIT = 56 * 1024 * 1024

_NT = (((1,), (1,)), ((), ()))
_TN = (((0,), (0,)), ((), ()))


def _mm(a, b):
    return jnp.dot(a, b, preferred_element_type=F32)


def _mm_nt(a, b):
    return lax.dot_general(a, b, _NT, preferred_element_type=F32)


def _rms(x, w):
    return x * lax.rsqrt(jnp.mean(x * x, axis=-1, keepdims=True) + RMS_EPS) * w


def _silu(x):
    return x * jax.nn.sigmoid(x)


def _params(*sem):
    return pltpu.CompilerParams(dimension_semantics=sem, vmem_limit_bytes=VMEM_LIMIT)


def _full(shape):
    n = len(shape)
    return pl.BlockSpec(shape, lambda *_: (0,) * n)


_PROJ_WIDTHS = (NSA_WIDTH, KV_WIDTH, KV_WIDTH, 2 * KV_WIDTH, 2 * KV_WIDTH, LANE, LANE,
                SSD_WIDTH, SSD_XBC)


def _pack_w_in(w):
    q, kc, vc, ks, vs, kw, vw, gl, z, xbc, dt = jnp.split(w, IN_SPLITS, axis=-1)
    pad = lambda a: jnp.pad(a, ((0, 0), (0, LANE - a.shape[1])))
    return jnp.concatenate([q, kc, vc, ks, vs, kw, vw, pad(gl), pad(dt), z, xbc], axis=-1).astype(BF16)


def _in_proj_body(x_ref, nw_ref, w_ref, *out_refs):
    h = _rms(x_ref[...], nw_ref[...]).astype(BF16)
    off = 0
    pending = list(out_refs)
    while pending:
        n_out = 2 if (pending[0].shape[-1] == LANE and len(pending) > 1 and pending[1].shape[-1] == LANE) else 1
        refs, pending = pending[:n_out], pending[n_out:]
        n = sum(r.shape[-1] for r in refs)
        res = _mm(h, w_ref[:, off:off + n])
        col = 0
        for r in refs:
            r[...] = res[:, col:col + r.shape[-1]]
            col += r.shape[-1]
        off += n


def _in_proj(x2, nw, w_packed, tm=512):
    t = x2.shape[0]
    ncol = w_packed.shape[1]
    return pl.pallas_call(
        _in_proj_body,
        grid=(t // tm,),
        in_specs=[pl.BlockSpec((tm, D_MODEL), lambda i: (i, 0)),
                  _full((1, D_MODEL)), _full((D_MODEL, ncol))],
        out_specs=[pl.BlockSpec((tm, n), lambda i: (i, 0)) for n in _PROJ_WIDTHS],
        out_shape=[jax.ShapeDtypeStruct((t, n), F32) for n in _PROJ_WIDTHS],
        compiler_params=_params("parallel"),
        name="in_proj",
    )(x2, nw, w_packed)


def _pack_compress(pos, w1, b1, w2):
    half = CMP_BLOCK // 2
    g = NSA_KV_GROUPS
    eye = jnp.eye(g, dtype=F32)
    w1r = w1.reshape(CMP_BLOCK, NSA_HEAD_DIM, CMP_HIDDEN)
    w1e = jnp.einsum('ldn,gh->lgdhn', w1r, eye).reshape(2, half * g * NSA_HEAD_DIM, g * CMP_HIDDEN)
    pose = jnp.broadcast_to(pos[:, None, :], (CMP_BLOCK, g, NSA_HEAD_DIM)).reshape(2, half * g * NSA_HEAD_DIM)
    b1e = jnp.tile(b1, g)[None, :]
    w2e = jnp.einsum('nd,gh->gnhd', w2, eye).reshape(g * CMP_HIDDEN, g * NSA_HEAD_DIM)
    return pose, w1e.astype(BF16), b1e, w2e.astype(BF16)


def _compress_body(kc_ref, vc_ref, pk_ref, wk1_ref, bk1_ref, wk2_ref,
                   pv_ref, wv1_ref, bv1_ref, wv2_ref, ko_ref, vo_ref):
    def one(r_ref, pos_ref, w1_ref, b1_ref, w2_ref, o_ref):
        r = r_ref[0]
        ncp = r.shape[0]
        lo = _mm((r + pos_ref[0:1, :]).astype(BF16), w1_ref[0])
        hi = _mm((r + pos_ref[1:2, :]).astype(BF16), w1_ref[1])
        hid = lo + pltpu.roll(hi, ncp - 1, 0) + b1_ref[...]
        hid = jax.nn.gelu(hid)
        o_ref[0] = _mm(hid.astype(BF16), w2_ref[...])

    one(kc_ref, pk_ref, wk1_ref, bk1_ref, wk2_ref, ko_ref)
    one(vc_ref, pv_ref, wv1_ref, bv1_ref, wv2_ref, vo_ref)


def _compress(kc3, vc3, pk, pv):
    b, ncp, wide = kc3.shape
    blk = pl.BlockSpec((1, ncp, wide), lambda i: (i, 0, 0))
    oblk = pl.BlockSpec((1, ncp, KV_WIDTH), lambda i: (i, 0, 0))
    wspecs = lambda p: [_full(a.shape) for a in p]
    return pl.pallas_call(
        _compress_body,
        grid=(b,),
        in_specs=[blk, blk] + wspecs(pk) + wspecs(pv),
        out_specs=[oblk, oblk],
        out_shape=[jax.ShapeDtypeStruct((b, ncp, KV_WIDTH), F32)] * 2,
        compiler_params=_params("parallel"),
        name="compress",
    )(kc3, vc3, *pk, *pv)


BIAS_ZERO, BIAS_CAUSAL, BIAS_WINLO, BIAS_NONE = range(4)
SEL_ROWS = 32
V_ROWS = 80
assert ATT_TQ == ATT_TK and WIN == 2 * ATT_TK and KV_WIDTH == LANE and 2 * NSA_HEAD_DIM == LANE


def _nsa_consts(s):
    ncp = s // CMP_STRIDE
    ns = s // SLC_BLOCK
    assert ns <= SEL_ROWS
    c0 = np.arange(ncp)[None, :] * CMP_STRIDE
    jj = np.arange(SEL_ROWS)[:, None]
    ovl_t = (c0 < jj * SLC_BLOCK + SLC_BLOCK) & (c0 + CMP_BLOCK > jj * SLC_BLOCK) & (jj < ns)
    onehot = np.zeros((s, LANE), np.float32)
    onehot[np.arange(s), NSA_HEAD_DIM + np.arange(s) // SLC_BLOCK] = 1.0
    k = np.arange(ATT_TK)[:, None]
    t = np.arange(ATT_TQ)[None, :]
    bias = np.stack([np.zeros((ATT_TK, ATT_TQ)),
                     np.where(k <= t, 0.0, NEG),
                     np.where(k > t, 0.0, NEG),
                     np.full((ATT_TK, ATT_TQ), NEG)]).astype(np.float32)
    return jnp.asarray(ovl_t.astype(np.float32), BF16), jnp.asarray(onehot), jnp.asarray(bias)


def _nsa_body(q_ref, kcmp_ref, vcmp_ref, kvs_ref, kvw_ref, gl_ref, ovl_ref, onehot_ref, bias_ref,
              nw_ref, o_ref, kaug_ref, vaug_ref, acc_ref, m_ref, s_ref, *, seq):
    tq, tk, hd, rep = ATT_TQ, ATT_TK, NSA_HEAD_DIM, NSA_REP
    j = pl.program_id(1)
    ntop = min(SLC_TOP, seq // SLC_BLOCK)
    ncp = kcmp_ref.shape[1]
    nq = rep * tq

    @pl.when(j == 0)
    def _():
        lane = lax.broadcasted_iota(jnp.int32, (tk, LANE), 1)
        pad_rows = lax.broadcasted_iota(jnp.int32, (V_ROWS - hd, seq), 0)
        for slot in range(2 * NSA_KV_GROUPS):
            vaug_ref[slot, hd:V_ROWS, :] = jnp.where(pad_rows == 0, 1.0, 0.0).astype(BF16)
        for br, kv_ref in enumerate((kvs_ref, kvw_ref)):
            def chunk(c, carry, br=br, kv_ref=kv_ref):
                r0 = pl.multiple_of(c * tk, tk)
                k128 = kv_ref[0, pl.ds(r0, tk), 0:KV_WIDTH]
                v_t = kv_ref[0, pl.ds(r0, tk), KV_WIDTH:2 * KV_WIDTH].T
                extra = onehot_ref[pl.ds(r0, tk), :] if br == 0 else 0.0
                for g in range(NSA_KV_GROUPS):
                    kk = k128 if g == 0 else pltpu.roll(k128, hd, 1)
                    kaug_ref[2 * br + g, pl.ds(r0, tk), :] = jnp.where(lane < hd, kk, extra).astype(BF16)
                    vaug_ref[2 * br + g, 0:hd, pl.ds(r0, tk)] = v_t[g * hd:(g + 1) * hd].astype(BF16)
                return carry
            lax.fori_loop(0, seq // tk, chunk, 0)

    def scores(kv_slot, kt, q_all):
        k0 = pl.multiple_of(kt * tk, tk)
        return _mm_nt(kaug_ref[kv_slot, pl.ds(k0, tk), :], q_all)

    def absorb(items):
        stats = []
        for chain, _, _, n in items:
            s_t = s_ref[n]
            m_old = m_ref[chain]
            m_new = jnp.maximum(m_old, jnp.max(s_t, axis=0, keepdims=True))
            stats.append((m_old, m_new, jnp.exp2(s_t - m_new).astype(BF16)))
        pvs = [_mm(vaug_ref[kv_slot, :, pl.ds(pl.multiple_of(kt * tk, tk), tk)], p)
               for (_, kv_slot, kt, _), (_, _, p) in zip(items, stats)]
        for (chain, _, _, _), (m_old, m_new, _), pv in zip(items, stats, pvs):
            acc_ref[chain] = jnp.exp2(m_old - m_new) * acc_ref[chain] + pv
            m_ref[chain] = m_new

    def run(tiles):
        for n, (chain, kv_slot, kt, q, bias) in enumerate(tiles):
            s_ref[n] = scores(kv_slot, kt, q) if bias is None else scores(kv_slot, kt, q) + bias
        items = [(chain, kv_slot, kt, n) for n, (chain, kv_slot, kt, q, bias) in enumerate(tiles)]
        for n in range(0, len(items), 2):
            absorb(items[n:n + 2])

    def finish(chain):
        acc = acc_ref[chain]
        return acc[0:hd] / acc[hd:hd + 1]

    lane_q = lax.broadcasted_iota(jnp.int32, (tq, LANE), 1)
    lane_c = lax.broadcasted_iota(jnp.int32, (ncp, LANE), 1)
    vc_t = vcmp_ref[0].T
    j_sub = lax.broadcasted_iota(jnp.int32, (SEL_ROWS, tq), 0)
    j8 = lax.broadcasted_iota(jnp.int32, (SUBLANE, tq), 0)
    ngrp = SEL_ROWS // SUBLANE
    groups = range(NSA_KV_GROUPS)

    def prepare(qt):
        t0 = (2 * j + qt) * tq
        qv = q_ref[0, qt * tq:(qt + 1) * tq, :]
        t_lane = t0 + (lax.broadcasted_iota(jnp.int32, (ncp, nq), 1) & (tq - 1))
        cmask = (lax.broadcasted_iota(jnp.int32, (ncp, nq), 0) * CMP_STRIDE + (CMP_BLOCK - 1)) <= t_lane
        cur = jnp.right_shift(t0 + lax.broadcasted_iota(jnp.int32, (SEL_ROWS, tq), 1),
                              int(math.log2(SLC_BLOCK)))
        bonus = jnp.where((j_sub == 0) | (j_sub == cur) | (j_sub == cur - 1), FORCE_BONUS, 0.0)
        valid = j_sub <= cur
        q_plain, q_slc, o_cmp = [], [], []
        for g in groups:
            heads = []
            for r in range(rep):
                hidx = g * rep + r
                c128 = qv[:, (hidx // 2) * LANE:(hidx // 2 + 1) * LANE]
                if hidx % 2:
                    c128 = pltpu.roll(c128, hd, 1)
                heads.append(jnp.where(lane_q < hd, c128 * (LOG2E * hd ** -0.5), 0.0))
            q_plain.append(jnp.concatenate(heads, axis=0).astype(BF16))

            kc = kcmp_ref[0]
            if g:
                kc = pltpu.roll(kc, hd, 1)
            kc = jnp.where(lane_c < hd, kc, 0.0).astype(BF16)
            s_t = jnp.where(cmask, _mm_nt(kc, q_plain[g]), NEG)
            e = jnp.exp2(s_t - jnp.max(s_t, axis=0, keepdims=True))
            p = jnp.where(cmask, e / jnp.sum(e, axis=0, keepdims=True), 0.0).astype(BF16)
            o_cmp.append(_mm(vc_t[g * hd:(g + 1) * hd].astype(BF16), p))
            imp4 = _mm(ovl_ref[...], p)
            imp = imp4[:, 0:tq]
            for r in range(1, rep):
                imp = imp + imp4[:, r * tq:(r + 1) * tq]

            score = jnp.where(valid, imp + bonus, NEG)
            grp = [score[SUBLANE * a:SUBLANE * (a + 1)] for a in range(ngrp)]
            rank = [jnp.zeros((SUBLANE, tq), F32) for _ in range(ngrp)]
            for k in range(SEL_ROWS):
                sk = jnp.broadcast_to(score[k:k + 1, :], (SUBLANE, tq))
                for a in range(ngrp):
                    if k < SUBLANE * a:
                        inc = jnp.where(sk >= grp[a], 1.0, 0.0)
                    elif k >= SUBLANE * (a + 1):
                        inc = jnp.where(sk > grp[a], 1.0, 0.0)
                    else:
                        inc = jnp.where(j8 > k - SUBLANE * a, jnp.where(sk >= grp[a], 1.0, 0.0),
                                        jnp.where(sk > grp[a], 1.0, 0.0))
                    rank[a] = rank[a] + inc
            bias_rows = [jnp.where(rk < ntop, 0.0, NEG) for rk in rank]
            b128 = jnp.concatenate([jnp.zeros((hd, tq), F32)] + bias_rows
                                   + [jnp.zeros((LANE - hd - SEL_ROWS, tq), F32)], axis=0)
            bq = b128.T
            q_slc.append(jnp.concatenate([h + bq for h in heads], axis=0).astype(BF16))
        return q_plain, q_slc, o_cmp

    prep = [prepare(qt) for qt in range(2)]
    q_plain = [p[0] for p in prep]
    q_slc = [p[1] for p in prep]

    chain = lambda br, qt, g: (br * 2 + qt) * NSA_KV_GROUPS + g
    m_ref[...] = jnp.full(m_ref.shape, NEG, F32)
    acc_ref[...] = jnp.zeros(acc_ref.shape, F32)
    tiled = lambda kind: jnp.concatenate([bias_ref[kind]] * rep, axis=1)

    def slc_step(t, carry):
        run([(chain(0, qt, g), g, 2 * t + u, q_slc[qt][g], None)
             for u in range(2) for qt in range(2) for g in groups])
        return carry

    lax.fori_loop(0, j, slc_step, 0)
    causal = tiled(BIAS_CAUSAL)
    run([(chain(0, 0, g), g, 2 * j, q_slc[0][g], causal) for g in groups]
        + [(chain(0, 1, g), g, 2 * j, q_slc[1][g], None) for g in groups]
        + [(chain(0, 1, g), g, 2 * j + 1, q_slc[1][g], causal) for g in groups])
    for qt in range(2):
        tiles = []
        for step, kind in enumerate((BIAS_WINLO, BIAS_ZERO, BIAS_CAUSAL)):
            kt = 2 * j + qt - 2 + step
            if step == 2:
                bias = causal
            elif step == 1 and qt == 1:
                bias = None
            else:
                bias = tiled(jnp.where(kt >= 0, kind, BIAS_NONE))
            tiles += [(chain(1, qt, g), 2 + g, jnp.maximum(kt, 0), q_plain[qt][g], bias) for g in groups]
        run(tiles)

    for qt in range(2):
        gates_t = jax.nn.sigmoid(gl_ref[0, qt * tq:(qt + 1) * tq, :]).T
        o_cmp = prep[qt][2]
        outs_t = []
        for g in groups:
            o_slc = finish(chain(0, qt, g))
            o_win = finish(chain(1, qt, g))
            for r in range(rep):
                c = (g * rep + r) * 3
                sl = slice(r * tq, (r + 1) * tq)
                outs_t.append(gates_t[c:c + 1] * o_cmp[g][:, sl] + gates_t[c + 1:c + 2] * o_slc[:, sl]
                              + gates_t[c + 2:c + 3] * o_win[:, sl])
        o_ref[0, qt * tq:(qt + 1) * tq, :] = _rms(jnp.concatenate(outs_t, axis=0).T, nw_ref[...]).astype(o_ref.dtype)


def _nsa(q3, kcmp, vcmp, kvs3, kvw3, gl3, consts, nw):
    b, s, _ = q3.shape
    tq = ATT_TQ
    ncp = kcmp.shape[1]
    row = lambda w: pl.BlockSpec((1, 2 * tq, w), lambda bi, i: (bi, i, 0))
    per_b = lambda n, w: pl.BlockSpec((1, n, w), lambda bi, i: (bi, 0, 0))
    return pl.pallas_call(
        functools.partial(_nsa_body, seq=s),
        grid=(b, s // (2 * tq)),
        in_specs=[row(NSA_WIDTH), per_b(ncp, KV_WIDTH), per_b(ncp, KV_WIDTH),
                  per_b(s, 2 * KV_WIDTH), per_b(s, 2 * KV_WIDTH), row(LANE)]
                 + [_full(c.shape) for c in consts] + [_full((1, NSA_WIDTH))],
        out_specs=row(NSA_WIDTH),
        out_shape=jax.ShapeDtypeStruct((b, s, NSA_WIDTH), BF16),
        scratch_shapes=[pltpu.VMEM((2 * NSA_KV_GROUPS, s, LANE), BF16),
                        pltpu.VMEM((2 * NSA_KV_GROUPS, V_ROWS, s), BF16),
                        pltpu.VMEM((4 * NSA_KV_GROUPS, V_ROWS, NSA_REP * tq), F32),
                        pltpu.VMEM((4 * NSA_KV_GROUPS, 1, NSA_REP * tq), F32),
                        pltpu.VMEM((4 * NSA_KV_GROUPS, ATT_TK, NSA_REP * tq), F32)],
        compiler_params=_params("parallel", "arbitrary"),
        name="nsa_attn",
    )(q3, kcmp, vcmp, kvs3, kvw3, gl3, *consts, nw)


def _shifted_rows(x, halo, sh):
    r = pltpu.roll(x, sh, 0)
    row = lax.broadcasted_iota(jnp.int32, (HALO, x.shape[1]), 0)
    top = jnp.where(row < sh, pltpu.roll(halo, sh, 0), r[0:HALO])
    return jnp.concatenate([top, r[HALO:]], axis=0)


def _ssd_body(z_ref, xbc_ref, dt_ref, dtt_ref, cw_ref, cb_ref, dtb_row_ref, dtb_col_ref,
              alog_row_ref, alog_col_ref, d_ref, nw_ref, y_ref, state_ref, halo_ref):
    L = SSD_CHUNK
    hd, nst = SSD_HEAD_DIM, SSD_STATE
    rep = SSD_HEADS // SSD_GROUPS
    gw = rep * hd

    @pl.when(pl.program_id(1) == 0)
    def _():
        state_ref[...] = jnp.zeros_like(state_ref)
        halo_ref[...] = jnp.zeros_like(halo_ref)

    xr = xbc_ref[0]
    halo = halo_ref[...]
    conv = cb_ref[...] + cw_ref[SSD_CONV - 1:SSD_CONV, :] * xr
    for sh in range(1, SSD_CONV):
        conv = conv + cw_ref[SSD_CONV - 1 - sh:SSD_CONV - sh, :] * _shifted_rows(xr, halo, sh)
    halo_ref[...] = xr[L - HALO:L, :]
    xc = _silu(conv)
    xs = xc[:, :SSD_WIDTH]
    bm = xc[:, SSD_WIDTH:SSD_WIDTH + SSD_GROUPS * nst]
    cm = xc[:, SSD_WIDTH + SSD_GROUPS * nst:]

    dt_col = jax.nn.softplus(dt_ref[0] + dtb_row_ref[...])
    a_col = dt_col * (-jnp.exp(alog_row_ref[...]))
    i0 = lax.broadcasted_iota(jnp.int32, (L, L), 0)
    i1 = lax.broadcasted_iota(jnp.int32, (L, L), 1)
    tri = i0 >= i1
    acs_col = jnp.dot(jnp.where(tri, 1.0, 0.0), a_col, precision=lax.Precision.HIGHEST,
                      preferred_element_type=F32)
    dt_row = jax.nn.softplus(dtt_ref[0] + dtb_col_ref[...])
    a_row = dt_row * (-jnp.exp(alog_col_ref[...]))
    acs_row = jnp.dot(a_row, jnp.where(i0 <= i1, 1.0, 0.0), precision=lax.Precision.HIGHEST,
                      preferred_element_type=F32)
    acs_last = acs_col[L - 1:L, :]

    ys = []
    for g in range(SSD_GROUPS):
        bg = bm[:, g * nst:(g + 1) * nst].astype(BF16)
        cg = cm[:, g * nst:(g + 1) * nst].astype(BF16)
        cb = _mm_nt(cg, bg)
        state = state_ref[g]
        y_off = _mm(cg, state.astype(BF16))
        sx, cdec = [], []
        for r in range(rep):
            h = g * rep + r
            col = acs_col[:, h:h + 1]
            seg = col - acs_row[h:h + 1, :]
            xh = xs[:, h * hd:(h + 1) * hd]
            xdt = xh * dt_col[:, h:h + 1]
            y = _mm(jnp.where(tri, cb * jnp.exp(seg), 0.0).astype(BF16), xdt.astype(BF16))
            y = y + jnp.exp(col) * y_off[:, r * hd:(r + 1) * hd] + d_ref[:, h * hd:(h + 1) * hd] * xh
            ys.append(y)
            last = acs_last[:, h:h + 1]
            sx.append(xdt * jnp.exp(last - col))
            cdec.append(jnp.broadcast_to(jnp.exp(last), (1, hd)))
        sxg = jnp.concatenate(sx, axis=1).astype(BF16)
        new = lax.dot_general(bg, sxg, _TN, preferred_element_type=F32)
        state_ref[g] = state * jnp.concatenate(cdec, axis=1) + new
    y = jnp.concatenate(ys, axis=1) * _silu(z_ref[0])
    outs = []
    for g in range(SSD_GROUPS):
        outs.append(_rms(y[:, g * gw:(g + 1) * gw], nw_ref[:, g * gw:(g + 1) * gw]))
    y_ref[0] = jnp.concatenate(outs, axis=1).astype(y_ref.dtype)


def _ssd(z3, xbc3, dt3, dtt3, cw, cb, dtb_row, dtb_col, alog_row, alog_col, d_exp, nw):
    b, s, _ = z3.shape
    L = SSD_CHUNK
    row = lambda w: pl.BlockSpec((1, L, w), lambda bi, c: (bi, c, 0))
    small = [cw, cb, dtb_row, dtb_col, alog_row, alog_col, d_exp, nw]
    return pl.pallas_call(
        _ssd_body,
        grid=(b, s // L),
        in_specs=[row(SSD_WIDTH), row(SSD_XBC), row(LANE),
                  pl.BlockSpec((1, SSD_HEADS, L), lambda bi, c: (bi, 0, c))]
                 + [_full(a.shape) for a in small],
        out_specs=row(SSD_WIDTH),
        out_shape=jax.ShapeDtypeStruct((b, s, SSD_WIDTH), BF16),
        scratch_shapes=[pltpu.VMEM((SSD_GROUPS, SSD_STATE, SSD_WIDTH // SSD_GROUPS), F32),
                        pltpu.VMEM((HALO, SSD_XBC), F32)],
        compiler_params=_params("parallel", "arbitrary"),
        name="ssd",
    )(z3, xbc3, dt3, dtt3, *small)


def _ffn_body(x_ref, a_ref, s_ref, wa_ref, ws_ref, nw_ref, wg_ref, wu_ref, cw_ref, cb_ref, wd_ref, fw_ref,
              o_ref, halo_ref, act_ref, *, final):
    tm = x_ref.shape[1]
    nf = wg_ref.shape[1] // FFN_TF

    @pl.when(pl.program_id(1) == 0)
    def _():
        halo_ref[...] = jnp.zeros_like(halo_ref)

    x = x_ref[0] + _mm(a_ref[0], wa_ref[...]) + _mm(s_ref[0], ws_ref[...])
    h = _rms(x, nw_ref[...]).astype(BF16)
    for f in range(nf):
        gp = _mm(h, wg_ref[:, f * FFN_TF:(f + 1) * FFN_TF])
        up = _mm(h, wu_ref[:, f * FFN_TF:(f + 1) * FFN_TF])
        halo = halo_ref[f]
        cw = cw_ref[f]
        gate = cb_ref[f] + cw[FFN_CONV - 1:FFN_CONV, :] * gp
        for sh in range(1, FFN_CONV):
            gate = gate + cw[FFN_CONV - 1 - sh:FFN_CONV - sh, :] * _shifted_rows(gp, halo, sh)
        halo_ref[f] = gp[tm - HALO:tm, :]
        act_ref[:, f * FFN_TF:(f + 1) * FFN_TF] = (_silu(gate) * up).astype(BF16)
    y = x + _mm(act_ref[...], wd_ref[...])
    o_ref[0] = _rms(y, fw_ref[...]) if final else y


def _ffn(x3, a3, s3, wa, ws, nw, wg, wu, cw, cb, wd, fw, final, tm=FFN_TM):
    b, s, _ = x3.shape
    nf = wg.shape[1] // FFN_TF
    row = lambda w: pl.BlockSpec((1, tm, w), lambda bi, i: (bi, i, 0))
    once = lambda a: pl.BlockSpec(a.shape, lambda *_: (0,) * a.ndim, pipeline_mode=pl.Buffered(1))
    return pl.pallas_call(
        functools.partial(_ffn_body, final=final),
        grid=(b, s // tm),
        in_specs=[row(D_MODEL), row(NSA_WIDTH), row(SSD_WIDTH), once(wa), once(ws), _full(nw.shape),
                  once(wg), once(wu), _full(cw.shape), _full(cb.shape), once(wd), _full(fw.shape)],
        out_specs=row(D_MODEL),
        out_shape=jax.ShapeDtypeStruct((b, s, D_MODEL), F32),
        scratch_shapes=[pltpu.VMEM((nf, HALO, FFN_TF), F32), pltpu.VMEM((tm, D_FF), BF16)],
        compiler_params=_params("parallel", "arbitrary"),
        name="conv_ffn",
    )(x3, a3, s3, wa, ws, nw, wg, wu, cw, cb, wd, fw)


def _pad_lanes(v):
    return jnp.pad(v, (0, LANE - v.shape[0]))[None, :]


def kernel(x, norm_mix_w, w_in, cmp_pos_k, cmp_w1_k, cmp_b1_k, cmp_w2_k, cmp_pos_v, cmp_w1_v,
           cmp_b1_v, cmp_w2_v, nsa_norm_w, ssd_conv_w, ssd_conv_b, ssd_dt_bias, ssd_A_log, ssd_D,
           ssd_norm_w, w_out, norm_ffn_w, w_gate, w_up, ffn_conv_w, ffn_conv_b, w_down, norm_final_w):
    b, s, d = x.shape
    depth = w_in.shape[0]
    t = b * s
    nf = D_FF // FFN_TF
    nsa_consts = _nsa_consts(s)
    x3 = x
    for i in range(depth):
        x2 = x3.reshape(t, d)
        q, kc, vc, kvs, kvw, gl, dt, z, xbc = _in_proj(x2, norm_mix_w[i][None, :], _pack_w_in(w_in[i]))
        ncp = s // CMP_STRIDE
        kcmp, vcmp = _compress(
            kc.reshape(b, ncp, CMP_STRIDE * KV_WIDTH), vc.reshape(b, ncp, CMP_STRIDE * KV_WIDTH),
            _pack_compress(cmp_pos_k[i], cmp_w1_k[i], cmp_b1_k[i], cmp_w2_k[i]),
            _pack_compress(cmp_pos_v[i], cmp_w1_v[i], cmp_b1_v[i], cmp_w2_v[i]))
        o_attn = _nsa(q.reshape(b, s, -1), kcmp, vcmp, kvs.reshape(b, s, -1), kvw.reshape(b, s, -1),
                      gl.reshape(b, s, -1), nsa_consts, nsa_norm_w[i][None, :])
        dt3 = dt.reshape(b, s, LANE)
        dtt3 = jnp.swapaxes(dt3[:, :, :SSD_HEADS], 1, 2)
        o_ssd = _ssd(z.reshape(b, s, -1), xbc.reshape(b, s, -1), dt3, dtt3,
                     ssd_conv_w[i], ssd_conv_b[i][None, :],
                     _pad_lanes(ssd_dt_bias[i]), ssd_dt_bias[i][:, None],
                     _pad_lanes(ssd_A_log[i]), ssd_A_log[i][:, None],
                     jnp.repeat(ssd_D[i], SSD_HEAD_DIM)[None, :], ssd_norm_w[i][None, :])
        wo = w_out[i].astype(BF16)
        x3 = _ffn(x3, o_attn, o_ssd, wo[:NSA_WIDTH], wo[NSA_WIDTH:], norm_ffn_w[i][None, :],
                  w_gate[i].astype(BF16), w_up[i].astype(BF16),
                  ffn_conv_w[i].reshape(FFN_CONV, nf, FFN_TF).transpose(1, 0, 2),
                  ffn_conv_b[i].reshape(nf, 1, FFN_TF),
                  w_down[i].astype(BF16), norm_final_w[None, :], final=(i == depth - 1))
    return x3
```

```python
import functools
import math

import numpy as np
import jax
import jax.numpy as jnp
from jax import lax
from jax.experimental import pallas as pl
from jax.experimental.pallas import tpu as pltpu

F32 = jnp.float32
BF16 = jnp.bfloat16

D_MODEL = 1024
NSA_HEADS = 8
NSA_KV_GROUPS = 2
NSA_HEAD_DIM = 64
NSA_REP = NSA_HEADS // NSA_KV_GROUPS
NSA_WIDTH = NSA_HEADS * NSA_HEAD_DIM
KV_WIDTH = NSA_KV_GROUPS * NSA_HEAD_DIM
CMP_BLOCK = 32
CMP_STRIDE = 16
CMP_HIDDEN = 128
SLC_BLOCK = 64
SLC_TOP = 16
WIN = 512
SSD_HEADS = 8
SSD_HEAD_DIM = 64
SSD_WIDTH = SSD_HEADS * SSD_HEAD_DIM
SSD_GROUPS = 2
SSD_STATE = 128
SSD_CONV = 4
SSD_CHUNK = 256
SSD_XBC = SSD_WIDTH + 2 * SSD_GROUPS * SSD_STATE
D_FF = 2816
FFN_CONV = 3
RMS_EPS = 1e-6
NEG = -1e30
LOG2E = math.log2(math.e)
FORCE_BONUS = 1e4

IN_SIZES = [NSA_WIDTH, KV_WIDTH, KV_WIDTH, KV_WIDTH, KV_WIDTH, KV_WIDTH, KV_WIDTH,
            3 * NSA_HEADS, SSD_WIDTH, SSD_XBC, SSD_HEADS]
IN_SPLITS = [int(v) for v in np.cumsum(IN_SIZES)[:-1]]

LANE = 128
SUBLANE = 8
HALO = SUBLANE
ATT_TQ = 256
ATT_TK = 256
FFN_TF = 256
FFN_TM = 1024
VMEM_LIMIT = 56 * 1024 * 1024

_NT = (((1,), (1,)), ((), ()))
_TN = (((0,), (0,)), ((), ()))


def _mm(a, b):
    return jnp.dot(a, b, preferred_element_type=F32)


def _mm_nt(a, b):
    return lax.dot_general(a, b, _NT, preferred_element_type=F32)


def _rms(x, w):
    return x * lax.rsqrt(jnp.mean(x * x, axis=-1, keepdims=True) + RMS_EPS) * w


def _silu(x):
    return x * jax.nn.sigmoid(x)


def _params(*sem):
    return pltpu.CompilerParams(dimension_semantics=sem, vmem_limit_bytes=VMEM_LIMIT)


def _full(shape):
    n = len(shape)
    return pl.BlockSpec(shape, lambda *_: (0,) * n)


_PROJ_WIDTHS = (NSA_WIDTH, KV_WIDTH, KV_WIDTH, 2 * KV_WIDTH, 2 * KV_WIDTH, LANE, LANE,
                SSD_WIDTH, SSD_XBC)


def _pack_w_in(w):
    q, kc, vc, ks, vs, kw, vw, gl, z, xbc, dt = jnp.split(w, IN_SPLITS, axis=-1)
    pad = lambda a: jnp.pad(a, ((0, 0), (0, LANE - a.shape[1])))
    return jnp.concatenate([q, kc, vc, ks, vs, kw, vw, pad(gl), pad(dt), z, xbc], axis=-1).astype(BF16)


def _in_proj_body(x_ref, nw_ref, w_ref, *out_refs):
    h = _rms(x_ref[...], nw_ref[...]).astype(BF16)
    off = 0
    pending = list(out_refs)
    while pending:
        n_out = 2 if (pending[0].shape[-1] == LANE and len(pending) > 1 and pending[1].shape[-1] == LANE) else 1
        refs, pending = pending[:n_out], pending[n_out:]
        n = sum(r.shape[-1] for r in refs)
        res = _mm(h, w_ref[:, off:off + n])
        col = 0
        for r in refs:
            r[...] = res[:, col:col + r.shape[-1]]
            col += r.shape[-1]
        off += n


def _in_proj(x2, nw, w_packed, tm=512):
    t = x2.shape[0]
    ncol = w_packed.shape[1]
    return pl.pallas_call(
        _in_proj_body,
        grid=(t // tm,),
        in_specs=[pl.BlockSpec((tm, D_MODEL), lambda i: (i, 0)),
                  _full((1, D_MODEL)), _full((D_MODEL, ncol))],
        out_specs=[pl.BlockSpec((tm, n), lambda i: (i, 0)) for n in _PROJ_WIDTHS],
        out_shape=[jax.ShapeDtypeStruct((t, n), F32) for n in _PROJ_WIDTHS],
        compiler_params=_params("parallel"),
        name="in_proj",
    )(x2, nw, w_packed)


def _pack_compress(pos, w1, b1, w2):
    half = CMP_BLOCK // 2
    g = NSA_KV_GROUPS
    eye = jnp.eye(g, dtype=F32)
    w1r = w1.reshape(CMP_BLOCK, NSA_HEAD_DIM, CMP_HIDDEN)
    w1e = jnp.einsum('ldn,gh->lgdhn', w1r, eye).reshape(2, half * g * NSA_HEAD_DIM, g * CMP_HIDDEN)
    pose = jnp.broadcast_to(pos[:, None, :], (CMP_BLOCK, g, NSA_HEAD_DIM)).reshape(2, half * g * NSA_HEAD_DIM)
    b1e = jnp.tile(b1, g)[None, :]
    w2e = jnp.einsum('nd,gh->gnhd', w2, eye).reshape(g * CMP_HIDDEN, g * NSA_HEAD_DIM)
    return pose, w1e.astype(BF16), b1e, w2e.astype(BF16)


def _compress_body(kc_ref, vc_ref, pk_ref, wk1_ref, bk1_ref, wk2_ref,
                   pv_ref, wv1_ref, bv1_ref, wv2_ref, ko_ref, vo_ref):
    def one(r_ref, pos_ref, w1_ref, b1_ref, w2_ref, o_ref):
        ncp = r_ref.shape[1] // CMP_STRIDE
        lo = jnp.zeros((ncp, w1_ref.shape[2]), F32)
        hi = jnp.zeros((ncp, w1_ref.shape[2]), F32)
        for l in range(CMP_STRIDE):
            cols = slice(l * KV_WIDTH, (l + 1) * KV_WIDTH)
            rl = r_ref[0, pl.ds(l, ncp, stride=CMP_STRIDE), :]
            lo = lo + _mm((rl + pos_ref[0:1, cols]).astype(BF16), w1_ref[0, cols, :])
            hi = hi + _mm((rl + pos_ref[1:2, cols]).astype(BF16), w1_ref[1, cols, :])
        hid = lo + pltpu.roll(hi, ncp - 1, 0) + b1_ref[...]
        hid = jax.nn.gelu(hid)
        o_ref[0] = _mm(hid.astype(BF16), w2_ref[...])

    one(kc_ref, pk_ref, wk1_ref, bk1_ref, wk2_ref, ko_ref)
    one(vc_ref, pv_ref, wv1_ref, bv1_ref, wv2_ref, vo_ref)


def _compress(kc3, vc3, pk, pv):
    b, seq, wide = kc3.shape
    ncp = seq // CMP_STRIDE
    blk = pl.BlockSpec((1, seq, wide), lambda i: (i, 0, 0))
    oblk = pl.BlockSpec((1, ncp, KV_WIDTH), lambda i: (i, 0, 0))
    wspecs = lambda p: [_full(a.shape) for a in p]
    return pl.pallas_call(
        _compress_body,
        grid=(b,),
        in_specs=[blk, blk] + wspecs(pk) + wspecs(pv),
        out_specs=[oblk, oblk],
        out_shape=[jax.ShapeDtypeStruct((b, ncp, KV_WIDTH), F32)] * 2,
        compiler_params=_params("parallel"),
        name="compress",
    )(kc3, vc3, *pk, *pv)


BIAS_ZERO, BIAS_CAUSAL, BIAS_WINLO, BIAS_NONE = range(4)
SEL_ROWS = 32
V_ROWS = 80
assert ATT_TQ == ATT_TK and WIN == 2 * ATT_TK and KV_WIDTH == LANE and 2 * NSA_HEAD_DIM == LANE


def _nsa_consts(s):
    ncp = s // CMP_STRIDE
    ns = s // SLC_BLOCK
    assert ns <= SEL_ROWS
    c0 = np.arange(ncp)[None, :] * CMP_STRIDE
    jj = np.arange(SEL_ROWS)[:, None]
    ovl_t = (c0 < jj * SLC_BLOCK + SLC_BLOCK) & (c0 + CMP_BLOCK > jj * SLC_BLOCK) & (jj < ns)
    onehot = np.zeros((s, LANE), np.float32)
    onehot[np.arange(s), NSA_HEAD_DIM + np.arange(s) // SLC_BLOCK] = 1.0
    k = np.arange(ATT_TK)[:, None]
    t = np.arange(ATT_TQ)[None, :]
    bias = np.stack([np.zeros((ATT_TK, ATT_TQ)),
                     np.where(k <= t, 0.0, NEG),
                     np.where(k > t, 0.0, NEG),
                     np.full((ATT_TK, ATT_TQ), NEG)]).astype(np.float32)
    return jnp.asarray(ovl_t.astype(np.float32), BF16), jnp.asarray(onehot), jnp.asarray(bias)


def _nsa_body(q_ref, kcmp_ref, vcmp_ref, kvs_ref, kvw_ref, gl_ref, ovl_ref, onehot_ref, bias_ref,
              nw_ref, o_ref, kaug_ref, vaug_ref, acc_ref, m_ref, s_ref, *, seq):
    tq, tk, hd, rep = ATT_TQ, ATT_TK, NSA_HEAD_DIM, NSA_REP
    j = pl.program_id(1)
    ntop = min(SLC_TOP, seq // SLC_BLOCK)
    ncp = kcmp_ref.shape[1]
    nq = rep * tq

    @pl.when(j == 0)
    def _():
        lane = lax.broadcasted_iota(jnp.int32, (tk, LANE), 1)
        pad_rows = lax.broadcasted_iota(jnp.int32, (V_ROWS - hd, seq), 0)
        for slot in range(2 * NSA_KV_GROUPS):
            vaug_ref[slot, hd:V_ROWS, :] = jnp.where(pad_rows == 0, 1.0, 0.0).astype(BF16)
        for br, kv_ref in enumerate((kvs_ref, kvw_ref)):
            def chunk(c, carry, br=br, kv_ref=kv_ref):
                r0 = pl.multiple_of(c * tk, tk)
                k128 = kv_ref[0, pl.ds(r0, tk), 0:KV_WIDTH]
                v_t = kv_ref[0, pl.ds(r0, tk), KV_WIDTH:2 * KV_WIDTH].T
                extra = onehot_ref[pl.ds(r0, tk), :] if br == 0 else 0.0
                for g in range(NSA_KV_GROUPS):
                    kk = k128 if g == 0 else pltpu.roll(k128, hd, 1)
                    kaug_ref[2 * br + g, pl.ds(r0, tk), :] = jnp.where(lane < hd, kk, extra).astype(BF16)
                    vaug_ref[2 * br + g, 0:hd, pl.ds(r0, tk)] = v_t[g * hd:(g + 1) * hd].astype(BF16)
                return carry
            lax.fori_loop(0, seq // tk, chunk, 0)

    def scores(kv_slot, kt, q_all):
        k0 = pl.multiple_of(kt * tk, tk)
        return _mm_nt(kaug_ref[kv_slot, pl.ds(k0, tk), :], q_all)

    def absorb(items):
        stats = []
        for chain, _, _, n in items:
            s_t = s_ref[n]
            m_old = m_ref[chain]
            m_new = jnp.maximum(m_old, jnp.max(s_t, axis=0, keepdims=True))
            stats.append((m_old, m_new, jnp.exp2(s_t - m_new).astype(BF16)))
        pvs = [_mm(vaug_ref[kv_slot, :, pl.ds(pl.multiple_of(kt * tk, tk), tk)], p)
               for (_, kv_slot, kt, _), (_, _, p) in zip(items, stats)]
        for (chain, _, _, _), (m_old, m_new, _), pv in zip(items, stats, pvs):
            acc_ref[chain] = jnp.exp2(m_old - m_new) * acc_ref[chain] + pv
            m_ref[chain] = m_new

    def run(tiles):
        for n, (chain, kv_slot, kt, q, bias) in enumerate(tiles):
            s_ref[n] = scores(kv_slot, kt, q) if bias is None else scores(kv_slot, kt, q) + bias
        items = [(chain, kv_slot, kt, n) for n, (chain, kv_slot, kt, q, bias) in enumerate(tiles)]
        for n in range(0, len(items), 2):
            absorb(items[n:n + 2])

    def finish(chain):
        acc = acc_ref[chain]
        return acc[0:hd] / acc[hd:hd + 1]

    lane_q = lax.broadcasted_iota(jnp.int32, (tq, LANE), 1)
    lane_c = lax.broadcasted_iota(jnp.int32, (ncp, LANE), 1)
    vc_t = vcmp_ref[0].T
    j_sub = lax.broadcasted_iota(jnp.int32, (SEL_ROWS, tq), 0)
    j8 = lax.broadcasted_iota(jnp.int32, (SUBLANE, tq), 0)
    ngrp = SEL_ROWS // SUBLANE
    groups = range(NSA_KV_GROUPS)

    def prepare(qt):
        t0 = (2 * j + qt) * tq
        qv = q_ref[0, qt * tq:(qt + 1) * tq, :]
        t_lane = t0 + (lax.broadcasted_iota(jnp.int32, (ncp, nq), 1) & (tq - 1))
        cmask = (lax.broadcasted_iota(jnp.int32, (ncp, nq), 0) * CMP_STRIDE + (CMP_BLOCK - 1)) <= t_lane
        cur = jnp.right_shift(t0 + lax.broadcasted_iota(jnp.int32, (SEL_ROWS, tq), 1),
                              int(math.log2(SLC_BLOCK)))
        bonus = jnp.where((j_sub == 0) | (j_sub == cur) | (j_sub == cur - 1), FORCE_BONUS, 0.0)
        valid = j_sub <= cur
        q_plain, q_slc, o_cmp = [], [], []
        for g in groups:
            heads = []
            for r in range(rep):
                hidx = g * rep + r
                c128 = qv[:, (hidx // 2) * LANE:(hidx // 2 + 1) * LANE]
                if hidx % 2:
                    c128 = pltpu.roll(c128, hd, 1)
                heads.append(jnp.where(lane_q < hd, c128 * (LOG2E * hd ** -0.5), 0.0))
            q_plain.append(jnp.concatenate(heads, axis=0).astype(BF16))

            kc = kcmp_ref[0]
            if g:
                kc = pltpu.roll(kc, hd, 1)
            kc = jnp.where(lane_c < hd, kc, 0.0).astype(BF16)
            s_t = jnp.where(cmask, _mm_nt(kc, q_plain[g]), NEG)
            e = jnp.exp2(s_t - jnp.max(s_t, axis=0, keepdims=True))
            p = jnp.where(cmask, e / jnp.sum(e, axis=0, keepdims=True), 0.0).astype(BF16)
            o_cmp.append(_mm(vc_t[g * hd:(g + 1) * hd].astype(BF16), p))
            imp4 = _mm(ovl_ref[...], p)
            imp = imp4[:, 0:tq]
            for r in range(1, rep):
                imp = imp + imp4[:, r * tq:(r + 1) * tq]

            score = jnp.where(valid, imp + bonus, NEG)
            grp = [score[SUBLANE * a:SUBLANE * (a + 1)] for a in range(ngrp)]
            rank = [jnp.zeros((SUBLANE, tq), F32) for _ in range(ngrp)]
            for k in range(SEL_ROWS):
                sk = jnp.broadcast_to(score[k:k + 1, :], (SUBLANE, tq))
                for a in range(ngrp):
                    if k < SUBLANE * a:
                        inc = jnp.where(sk >= grp[a], 1.0, 0.0)
                    elif k >= SUBLANE * (a + 1):
                        inc = jnp.where(sk > grp[a], 1.0, 0.0)
                    else:
                        inc = jnp.where(j8 > k - SUBLANE * a, jnp.where(sk >= grp[a], 1.0, 0.0),
                                        jnp.where(sk > grp[a], 1.0, 0.0))
                    rank[a] = rank[a] + inc
            bias_rows = [jnp.where(rk < ntop, 0.0, NEG) for rk in rank]
            b128 = jnp.concatenate([jnp.zeros((hd, tq), F32)] + bias_rows
                                   + [jnp.zeros((LANE - hd - SEL_ROWS, tq), F32)], axis=0)
            bq = b128.T
            q_slc.append(jnp.concatenate([h + bq for h in heads], axis=0).astype(BF16))
        return q_plain, q_slc, o_cmp

    prep = [prepare(qt) for qt in range(2)]
    q_plain = [p[0] for p in prep]
    q_slc = [p[1] for p in prep]

    chain = lambda br, qt, g: (br * 2 + qt) * NSA_KV_GROUPS + g
    m_ref[...] = jnp.full(m_ref.shape, NEG, F32)
    acc_ref[...] = jnp.zeros(acc_ref.shape, F32)
    tiled = lambda kind: jnp.concatenate([bias_ref[kind]] * rep, axis=1)

    def slc_step(t, carry):
        run([(chain(0, qt, g), g, 2 * t + u, q_slc[qt][g], None)
             for u in range(2) for qt in range(2) for g in groups])
        return carry

    lax.fori_loop(0, j, slc_step, 0)
    causal = tiled(BIAS_CAUSAL)
    run([(chain(0, 0, g), g, 2 * j, q_slc[0][g], causal) for g in groups]
        + [(chain(0, 1, g), g, 2 * j, q_slc[1][g], None) for g in groups]
        + [(chain(0, 1, g), g, 2 * j + 1, q_slc[1][g], causal) for g in groups])
    for qt in range(2):
        tiles = []
        for step, kind in enumerate((BIAS_WINLO, BIAS_ZERO, BIAS_CAUSAL)):
            kt = 2 * j + qt - 2 + step
            if step == 2:
                bias = causal
            elif step == 1 and qt == 1:
                bias = None
            else:
                bias = tiled(jnp.where(kt >= 0, kind, BIAS_NONE))
            tiles += [(chain(1, qt, g), 2 + g, jnp.maximum(kt, 0), q_plain[qt][g], bias) for g in groups]
        run(tiles)

    for qt in range(2):
        gates_t = jax.nn.sigmoid(gl_ref[0, qt * tq:(qt + 1) * tq, :]).T
        o_cmp = prep[qt][2]
        outs_t = []
        for g in groups:
            o_slc = finish(chain(0, qt, g))
            o_win = finish(chain(1, qt, g))
            for r in range(rep):
                c = (g * rep + r) * 3
                sl = slice(r * tq, (r + 1) * tq)
                outs_t.append(gates_t[c:c + 1] * o_cmp[g][:, sl] + gates_t[c + 1:c + 2] * o_slc[:, sl]
                              + gates_t[c + 2:c + 3] * o_win[:, sl])
        o_ref[0, qt * tq:(qt + 1) * tq, :] = _rms(jnp.concatenate(outs_t, axis=0).T, nw_ref[...]).astype(o_ref.dtype)


def _nsa(q3, kcmp, vcmp, kvs3, kvw3, gl3, consts, nw):
    b, s, _ = q3.shape
    tq = ATT_TQ
    ncp = kcmp.shape[1]
    row = lambda w: pl.BlockSpec((1, 2 * tq, w), lambda bi, i: (bi, i, 0))
    per_b = lambda n, w: pl.BlockSpec((1, n, w), lambda bi, i: (bi, 0, 0))
    return pl.pallas_call(
        functools.partial(_nsa_body, seq=s),
        grid=(b, s // (2 * tq)),
        in_specs=[row(NSA_WIDTH), per_b(ncp, KV_WIDTH), per_b(ncp, KV_WIDTH),
                  per_b(s, 2 * KV_WIDTH), per_b(s, 2 * KV_WIDTH), row(LANE)]
                 + [_full(c.shape) for c in consts] + [_full((1, NSA_WIDTH))],
        out_specs=row(NSA_WIDTH),
        out_shape=jax.ShapeDtypeStruct((b, s, NSA_WIDTH), BF16),
        scratch_shapes=[pltpu.VMEM((2 * NSA_KV_GROUPS, s, LANE), BF16),
                        pltpu.VMEM((2 * NSA_KV_GROUPS, V_ROWS, s), BF16),
                        pltpu.VMEM((4 * NSA_KV_GROUPS, V_ROWS, NSA_REP * tq), F32),
                        pltpu.VMEM((4 * NSA_KV_GROUPS, 1, NSA_REP * tq), F32),
                        pltpu.VMEM((4 * NSA_KV_GROUPS, ATT_TK, NSA_REP * tq), F32)],
        compiler_params=_params("parallel", "arbitrary"),
        name="nsa_attn",
    )(q3, kcmp, vcmp, kvs3, kvw3, gl3, *consts, nw)


def _shifted_rows(x, halo, sh):
    r = pltpu.roll(x, sh, 0)
    row = lax.broadcasted_iota(jnp.int32, (HALO, x.shape[1]), 0)
    top = jnp.where(row < sh, pltpu.roll(halo, sh, 0), r[0:HALO])
    return jnp.concatenate([top, r[HALO:]], axis=0)


def _ssd_body(z_ref, xbc_ref, dt_ref, dtt_ref, cw_ref, cb_ref, dtb_row_ref, dtb_col_ref,
              alog_row_ref, alog_col_ref, d_ref, nw_ref, y_ref, state_ref, halo_ref):
    L = SSD_CHUNK
    hd, nst = SSD_HEAD_DIM, SSD_STATE
    rep = SSD_HEADS // SSD_GROUPS
    gw = rep * hd

    @pl.when(pl.program_id(1) == 0)
    def _():
        state_ref[...] = jnp.zeros_like(state_ref)
        halo_ref[...] = jnp.zeros_like(halo_ref)

    xr = xbc_ref[0]
    halo = halo_ref[...]
    conv = cb_ref[...] + cw_ref[SSD_CONV - 1:SSD_CONV, :] * xr
    for sh in range(1, SSD_CONV):
        conv = conv + cw_ref[SSD_CONV - 1 - sh:SSD_CONV - sh, :] * _shifted_rows(xr, halo, sh)
    halo_ref[...] = xr[L - HALO:L, :]
    xc = _silu(conv)
    xs = xc[:, :SSD_WIDTH]
    bm = xc[:, SSD_WIDTH:SSD_WIDTH + SSD_GROUPS * nst]
    cm = xc[:, SSD_WIDTH + SSD_GROUPS * nst:]

    dt_col = jax.nn.softplus(dt_ref[0] + dtb_row_ref[...])
    a_col = dt_col * (-jnp.exp(alog_row_ref[...]))
    i0 = lax.broadcasted_iota(jnp.int32, (L, L), 0)
    i1 = lax.broadcasted_iota(jnp.int32, (L, L), 1)
    tri = i0 >= i1
    acs_col = jnp.dot(jnp.where(tri, 1.0, 0.0), a_col, precision=lax.Precision.HIGHEST,
                      preferred_element_type=F32)
    dt_row = jax.nn.softplus(dtt_ref[0] + dtb_col_ref[...])
    a_row = dt_row * (-jnp.exp(alog_col_ref[...]))
    acs_row = jnp.dot(a_row, jnp.where(i0 <= i1, 1.0, 0.0), precision=lax.Precision.HIGHEST,
                      preferred_element_type=F32)
    acs_last = acs_col[L - 1:L, :]

    ys = []
    for g in range(SSD_GROUPS):
        bg = bm[:, g * nst:(g + 1) * nst].astype(BF16)
        cg = cm[:, g * nst:(g + 1) * nst].astype(BF16)
        cb = _mm_nt(cg, bg)
        state = state_ref[g]
        y_off = _mm(cg, state.astype(BF16))
        sx, cdec = [], []
        for r in range(rep):
            h = g * rep + r
            col = acs_col[:, h:h + 1]
            seg = col - acs_row[h:h + 1, :]
            xh = xs[:, h * hd:(h + 1) * hd]
            xdt = xh * dt_col[:, h:h + 1]
            y = _mm(jnp.where(tri, cb * jnp.exp(seg), 0.0).astype(BF16), xdt.astype(BF16))
            y = y + jnp.exp(col) * y_off[:, r * hd:(r + 1) * hd] + d_ref[:, h * hd:(h + 1) * hd] * xh
            ys.append(y)
            last = acs_last[:, h:h + 1]
            sx.append(xdt * jnp.exp(last - col))
            cdec.append(jnp.broadcast_to(jnp.exp(last), (1, hd)))
        sxg = jnp.concatenate(sx, axis=1).astype(BF16)
        new = lax.dot_general(bg, sxg, _TN, preferred_element_type=F32)
        state_ref[g] = state * jnp.concatenate(cdec, axis=1) + new
    y = jnp.concatenate(ys, axis=1) * _silu(z_ref[0])
    outs = []
    for g in range(SSD_GROUPS):
        outs.append(_rms(y[:, g * gw:(g + 1) * gw], nw_ref[:, g * gw:(g + 1) * gw]))
    y_ref[0] = jnp.concatenate(outs, axis=1).astype(y_ref.dtype)


def _ssd(z3, xbc3, dt3, dtt3, cw, cb, dtb_row, dtb_col, alog_row, alog_col, d_exp, nw):
    b, s, _ = z3.shape
    L = SSD_CHUNK
    row = lambda w: pl.BlockSpec((1, L, w), lambda bi, c: (bi, c, 0))
    small = [cw, cb, dtb_row, dtb_col, alog_row, alog_col, d_exp, nw]
    return pl.pallas_call(
        _ssd_body,
        grid=(b, s // L),
        in_specs=[row(SSD_WIDTH), row(SSD_XBC), row(LANE),
                  pl.BlockSpec((1, SSD_HEADS, L), lambda bi, c: (bi, 0, c))]
                 + [_full(a.shape) for a in small],
        out_specs=row(SSD_WIDTH),
        out_shape=jax.ShapeDtypeStruct((b, s, SSD_WIDTH), BF16),
        scratch_shapes=[pltpu.VMEM((SSD_GROUPS, SSD_STATE, SSD_WIDTH // SSD_GROUPS), F32),
                        pltpu.VMEM((HALO, SSD_XBC), F32)],
        compiler_params=_params("parallel", "arbitrary"),
        name="ssd",
    )(z3, xbc3, dt3, dtt3, *small)


def _ffn_body(x_ref, a_ref, s_ref, wa_ref, ws_ref, nw_ref, wg_ref, wu_ref, cw_ref, cb_ref, wd_ref, fw_ref,
              o_ref, halo_ref, act_ref, *, final):
    tm = x_ref.shape[1]
    nf = wg_ref.shape[1] // FFN_TF

    @pl.when(pl.program_id(1) == 0)
    def _():
        halo_ref[...] = jnp.zeros_like(halo_ref)

    x = x_ref[0] + _mm(a_ref[0], wa_ref[...]) + _mm(s_ref[0], ws_ref[...])
    h = _rms(x, nw_ref[...]).astype(BF16)
    for f in range(nf):
        gp = _mm(h, wg_ref[:, f * FFN_TF:(f + 1) * FFN_TF])
        up = _mm(h, wu_ref[:, f * FFN_TF:(f + 1) * FFN_TF])
        halo = halo_ref[f]
        cw = cw_ref[f]
        gate = cb_ref[f] + cw[FFN_CONV - 1:FFN_CONV, :] * gp
        for sh in range(1, FFN_CONV):
            gate = gate + cw[FFN_CONV - 1 - sh:FFN_CONV - sh, :] * _shifted_rows(gp, halo, sh)
        halo_ref[f] = gp[tm - HALO:tm, :]
        act_ref[:, f * FFN_TF:(f + 1) * FFN_TF] = (_silu(gate) * up).astype(BF16)
    y = x + _mm(act_ref[...], wd_ref[...])
    o_ref[0] = _rms(y, fw_ref[...]) if final else y


def _ffn(x3, a3, s3, wa, ws, nw, wg, wu, cw, cb, wd, fw, final, tm=FFN_TM):
    b, s, _ = x3.shape
    nf = wg.shape[1] // FFN_TF
    row = lambda w: pl.BlockSpec((1, tm, w), lambda bi, i: (bi, i, 0))
    once = lambda a: pl.BlockSpec(a.shape, lambda *_: (0,) * a.ndim, pipeline_mode=pl.Buffered(1))
    return pl.pallas_call(
        functools.partial(_ffn_body, final=final),
        grid=(b, s // tm),
        in_specs=[row(D_MODEL), row(NSA_WIDTH), row(SSD_WIDTH), once(wa), once(ws), _full(nw.shape),
                  once(wg), once(wu), _full(cw.shape), _full(cb.shape), once(wd), _full(fw.shape)],
        out_specs=row(D_MODEL),
        out_shape=jax.ShapeDtypeStruct((b, s, D_MODEL), F32),
        scratch_shapes=[pltpu.VMEM((nf, HALO, FFN_TF), F32), pltpu.VMEM((tm, D_FF), BF16)],
        compiler_params=_params("parallel", "arbitrary"),
        name="conv_ffn",
    )(x3, a3, s3, wa, ws, nw, wg, wu, cw, cb, wd, fw)


def _pad_lanes(v):
    return jnp.pad(v, (0, LANE - v.shape[0]))[None, :]


def kernel(x, norm_mix_w, w_in, cmp_pos_k, cmp_w1_k, cmp_b1_k, cmp_w2_k, cmp_pos_v, cmp_w1_v,
           cmp_b1_v, cmp_w2_v, nsa_norm_w, ssd_conv_w, ssd_conv_b, ssd_dt_bias, ssd_A_log, ssd_D,
           ssd_norm_w, w_out, norm_ffn_w, w_gate, w_up, ffn_conv_w, ffn_conv_b, w_down, norm_final_w):
    b, s, d = x.shape
    depth = w_in.shape[0]
    t = b * s
    nf = D_FF // FFN_TF
    nsa_consts = _nsa_consts(s)
    x3 = x
    for i in range(depth):
        x2 = x3.reshape(t, d)
        q, kc, vc, kvs, kvw, gl, dt, z, xbc = _in_proj(x2, norm_mix_w[i][None, :], _pack_w_in(w_in[i]))
        ncp = s // CMP_STRIDE
        kcmp, vcmp = _compress(
            kc.reshape(b, s, KV_WIDTH), vc.reshape(b, s, KV_WIDTH),
            _pack_compress(cmp_pos_k[i], cmp_w1_k[i], cmp_b1_k[i], cmp_w2_k[i]),
            _pack_compress(cmp_pos_v[i], cmp_w1_v[i], cmp_b1_v[i], cmp_w2_v[i]))
        o_attn = _nsa(q.reshape(b, s, -1), kcmp, vcmp, kvs.reshape(b, s, -1), kvw.reshape(b, s, -1),
                      gl.reshape(b, s, -1), nsa_consts, nsa_norm_w[i][None, :])
        dt3 = dt.reshape(b, s, LANE)
        dtt3 = jnp.swapaxes(dt3[:, :, :SSD_HEADS], 1, 2)
        o_ssd = _ssd(z.reshape(b, s, -1), xbc.reshape(b, s, -1), dt3, dtt3,
                     ssd_conv_w[i], ssd_conv_b[i][None, :],
                     _pad_lanes(ssd_dt_bias[i]), ssd_dt_bias[i][:, None],
                     _pad_lanes(ssd_A_log[i]), ssd_A_log[i][:, None],
                     jnp.repeat(ssd_D[i], SSD_HEAD_DIM)[None, :], ssd_norm_w[i][None, :])
        wo = w_out[i].astype(BF16)
        x3 = _ffn(x3, o_attn, o_ssd, wo[:NSA_WIDTH], wo[NSA_WIDTH:], norm_ffn_w[i][None, :],
                  w_gate[i].astype(BF16), w_up[i].astype(BF16),
                  ffn_conv_w[i].reshape(FFN_CONV, nf, FFN_TF).transpose(1, 0, 2),
                  ffn_conv_b[i].reshape(nf, 1, FFN_TF),
                  w_down[i].astype(BF16), norm_final_w[None, :], final=(i == depth - 1))
    return x3
```

```python
import functools
import math

import numpy as np
import jax
import jax.numpy as jnp
from jax import lax
from jax.experimental import pallas as pl
from jax.experimental.pallas import tpu as pltpu

F32 = jnp.float32
BF16 = jnp.bfloat16

D_MODEL = 1024
NSA_HEADS = 8
NSA_KV_GROUPS = 2
NSA_HEAD_DIM = 64
NSA_REP = NSA_HEADS // NSA_KV_GROUPS
NSA_WIDTH = NSA_HEADS * NSA_HEAD_DIM
KV_WIDTH = NSA_KV_GROUPS * NSA_HEAD_DIM
CMP_BLOCK = 32
CMP_STRIDE = 16
CMP_HIDDEN = 128
SLC_BLOCK = 64
SLC_TOP = 16
WIN = 512
SSD_HEADS = 8
SSD_HEAD_DIM = 64
SSD_WIDTH = SSD_HEADS * SSD_HEAD_DIM
SSD_GROUPS = 2
SSD_STATE = 128
SSD_CONV = 4
SSD_CHUNK = 256
SSD_XBC = SSD_WIDTH + 2 * SSD_GROUPS * SSD_STATE
D_FF = 2816
FFN_CONV = 3
RMS_EPS = 1e-6
NEG = -1e30
LOG2E = math.log2(math.e)
FORCE_BONUS = 1e4

IN_SIZES = [NSA_WIDTH, KV_WIDTH, KV_WIDTH, KV_WIDTH, KV_WIDTH, KV_WIDTH, KV_WIDTH,
            3 * NSA_HEADS, SSD_WIDTH, SSD_XBC, SSD_HEADS]
IN_SPLITS = [int(v) for v in np.cumsum(IN_SIZES)[:-1]]

LANE = 128
SUBLANE = 8
HALO = SUBLANE
ATT_TQ = 256
ATT_TK = 256
FFN_TF = 256
FFN_TM = 1024
VMEM_LIMIT = 56 * 1024 * 1024

_NT = (((1,), (1,)), ((), ()))
_TN = (((0,), (0,)), ((), ()))


def _mm(a, b):
    return jnp.dot(a, b, preferred_element_type=F32)


def _mm_nt(a, b):
    return lax.dot_general(a, b, _NT, preferred_element_type=F32)


def _rms(x, w):
    return x * lax.rsqrt(jnp.mean(x * x, axis=-1, keepdims=True) + RMS_EPS) * w


def _silu(x):
    return x * jax.nn.sigmoid(x)


def _params(*sem):
    return pltpu.CompilerParams(dimension_semantics=sem, vmem_limit_bytes=VMEM_LIMIT)


def _full(shape):
    n = len(shape)
    return pl.BlockSpec(shape, lambda *_: (0,) * n)


_PROJ_WIDTHS = (NSA_WIDTH, KV_WIDTH, KV_WIDTH, 2 * KV_WIDTH, 2 * KV_WIDTH, LANE, LANE,
                SSD_WIDTH, SSD_XBC)


_PROJ_DTYPES = (BF16, F32, F32, BF16, BF16, F32, F32, F32, F32)


def _pack_w_in(w):
    q, kc, vc, ks, vs, kw, vw, gl, z, xbc, dt = jnp.split(w, IN_SPLITS, axis=-1)
    pad = lambda a: jnp.pad(a, ((0, 0), (0, LANE - a.shape[1])))
    return jnp.concatenate([q, kc, vc, ks, vs, kw, vw, pad(gl), pad(dt), z, xbc], axis=-1).astype(BF16)


def _in_proj_body(x_ref, nw_ref, w_ref, *out_refs):
    h = _rms(x_ref[...], nw_ref[...]).astype(BF16)
    off = 0
    pending = list(out_refs)
    while pending:
        n_out = 2 if (pending[0].shape[-1] == LANE and len(pending) > 1 and pending[1].shape[-1] == LANE) else 1
        refs, pending = pending[:n_out], pending[n_out:]
        n = sum(r.shape[-1] for r in refs)
        res = _mm(h, w_ref[:, off:off + n])
        col = 0
        for r in refs:
            r[...] = res[:, col:col + r.shape[-1]].astype(r.dtype)
            col += r.shape[-1]
        off += n


def _in_proj(x2, nw, w_packed, tm=512):
    t = x2.shape[0]
    ncol = w_packed.shape[1]
    return pl.pallas_call(
        _in_proj_body,
        grid=(t // tm,),
        in_specs=[pl.BlockSpec((tm, D_MODEL), lambda i: (i, 0)),
                  _full((1, D_MODEL)), _full((D_MODEL, ncol))],
        out_specs=[pl.BlockSpec((tm, n), lambda i: (i, 0)) for n in _PROJ_WIDTHS],
        out_shape=[jax.ShapeDtypeStruct((t, n), dt) for n, dt in zip(_PROJ_WIDTHS, _PROJ_DTYPES)],
        compiler_params=_params("parallel"),
        name="in_proj",
    )(x2, nw, w_packed)


def _pack_compress(pos, w1, b1, w2):
    half = CMP_BLOCK // 2
    g = NSA_KV_GROUPS
    eye = jnp.eye(g, dtype=F32)
    w1r = w1.reshape(CMP_BLOCK, NSA_HEAD_DIM, CMP_HIDDEN)
    w1e = jnp.einsum('ldn,gh->lgdhn', w1r, eye).reshape(2, half * g * NSA_HEAD_DIM, g * CMP_HIDDEN)
    pose = jnp.broadcast_to(pos[:, None, :], (CMP_BLOCK, g, NSA_HEAD_DIM)).reshape(2, half * g * NSA_HEAD_DIM)
    b1e = jnp.tile(b1, g)[None, :]
    w2e = jnp.einsum('nd,gh->gnhd', w2, eye).reshape(g * CMP_HIDDEN, g * NSA_HEAD_DIM)
    return pose, w1e.astype(BF16), b1e, w2e.astype(BF16)


def _compress_body(kc_ref, vc_ref, pk_ref, wk1_ref, bk1_ref, wk2_ref,
                   pv_ref, wv1_ref, bv1_ref, wv2_ref, ko_ref, vo_ref):
    def one(r_ref, pos_ref, w1_ref, b1_ref, w2_ref, o_ref):
        ncp = r_ref.shape[1] // CMP_STRIDE
        lo = jnp.zeros((ncp, w1_ref.shape[2]), F32)
        hi = jnp.zeros((ncp, w1_ref.shape[2]), F32)
        for l in range(CMP_STRIDE):
            cols = slice(l * KV_WIDTH, (l + 1) * KV_WIDTH)
            rl = r_ref[0, pl.ds(l, ncp, stride=CMP_STRIDE), :]
            lo = lo + _mm((rl + pos_ref[0:1, cols]).astype(BF16), w1_ref[0, cols, :])
            hi = hi + _mm((rl + pos_ref[1:2, cols]).astype(BF16), w1_ref[1, cols, :])
        hid = lo + pltpu.roll(hi, ncp - 1, 0) + b1_ref[...]
        hid = jax.nn.gelu(hid)
        o_ref[0] = _mm(hid.astype(BF16), w2_ref[...])

    one(kc_ref, pk_ref, wk1_ref, bk1_ref, wk2_ref, ko_ref)
    one(vc_ref, pv_ref, wv1_ref, bv1_ref, wv2_ref, vo_ref)


def _compress(kc3, vc3, pk, pv):
    b, seq, wide = kc3.shape
    ncp = seq // CMP_STRIDE
    blk = pl.BlockSpec((1, seq, wide), lambda i: (i, 0, 0))
    oblk = pl.BlockSpec((1, ncp, KV_WIDTH), lambda i: (i, 0, 0))
    wspecs = lambda p: [_full(a.shape) for a in p]
    return pl.pallas_call(
        _compress_body,
        grid=(b,),
        in_specs=[blk, blk] + wspecs(pk) + wspecs(pv),
        out_specs=[oblk, oblk],
        out_shape=[jax.ShapeDtypeStruct((b, ncp, KV_WIDTH), F32)] * 2,
        compiler_params=_params("parallel"),
        name="compress",
    )(kc3, vc3, *pk, *pv)


BIAS_ZERO, BIAS_CAUSAL, BIAS_WINLO, BIAS_NONE = range(4)
SEL_ROWS = 32
V_ROWS = 80
assert ATT_TQ == ATT_TK and WIN == 2 * ATT_TK and KV_WIDTH == LANE and 2 * NSA_HEAD_DIM == LANE


def _nsa_consts(s):
    ncp = s // CMP_STRIDE
    ns = s // SLC_BLOCK
    assert ns <= SEL_ROWS
    c0 = np.arange(ncp)[None, :] * CMP_STRIDE
    jj = np.arange(SEL_ROWS)[:, None]
    ovl_t = (c0 < jj * SLC_BLOCK + SLC_BLOCK) & (c0 + CMP_BLOCK > jj * SLC_BLOCK) & (jj < ns)
    onehot = np.zeros((s, LANE), np.float32)
    onehot[np.arange(s), NSA_HEAD_DIM + np.arange(s) // SLC_BLOCK] = 1.0
    k = np.arange(ATT_TK)[:, None]
    t = np.arange(ATT_TQ)[None, :]
    bias = np.stack([np.zeros((ATT_TK, ATT_TQ)),
                     np.where(k <= t, 0.0, NEG),
                     np.where(k > t, 0.0, NEG),
                     np.full((ATT_TK, ATT_TQ), NEG)]).astype(np.float32)
    return jnp.asarray(ovl_t.astype(np.float32), BF16), jnp.asarray(onehot), jnp.asarray(bias)


def _nsa_body(q_ref, kcmp_ref, vcmp_ref, kvs_ref, kvw_ref, gl_ref, ovl_ref, onehot_ref, bias_ref,
              nw_ref, o_ref, kaug_ref, vaug_ref, acc_ref, m_ref, s_ref, *, seq):
    tq, tk, hd, rep = ATT_TQ, ATT_TK, NSA_HEAD_DIM, NSA_REP
    j = pl.program_id(1)
    ntop = min(SLC_TOP, seq // SLC_BLOCK)
    ncp = kcmp_ref.shape[1]
    nq = rep * tq

    @pl.when(j == 0)
    def _():
        lane = lax.broadcasted_iota(jnp.int32, (tk, LANE), 1)
        pad_rows = lax.broadcasted_iota(jnp.int32, (V_ROWS - hd, seq), 0)
        for slot in range(2 * NSA_KV_GROUPS):
            vaug_ref[slot, hd:V_ROWS, :] = jnp.where(pad_rows == 0, 1.0, 0.0).astype(BF16)
        for br, kv_ref in enumerate((kvs_ref, kvw_ref)):
            def chunk(c, carry, br=br, kv_ref=kv_ref):
                r0 = pl.multiple_of(c * tk, tk)
                k128 = kv_ref[0, pl.ds(r0, tk), 0:KV_WIDTH].astype(F32)
                v_t = kv_ref[0, pl.ds(r0, tk), KV_WIDTH:2 * KV_WIDTH].astype(F32).T
                extra = onehot_ref[pl.ds(r0, tk), :] if br == 0 else 0.0
                for g in range(NSA_KV_GROUPS):
                    kk = k128 if g == 0 else pltpu.roll(k128, hd, 1)
                    kaug_ref[2 * br + g, pl.ds(r0, tk), :] = jnp.where(lane < hd, kk, extra).astype(BF16)
                    vaug_ref[2 * br + g, 0:hd, pl.ds(r0, tk)] = v_t[g * hd:(g + 1) * hd].astype(BF16)
                return carry
            lax.fori_loop(0, seq // tk, chunk, 0)

    def scores(kv_slot, kt, q_all):
        k0 = pl.multiple_of(kt * tk, tk)
        return _mm_nt(kaug_ref[kv_slot, pl.ds(k0, tk), :], q_all)

    def absorb(items):
        stats = []
        for chain, _, _, n in items:
            s_t = s_ref[n]
            m_old = m_ref[chain]
            m_new = jnp.maximum(m_old, jnp.max(s_t, axis=0, keepdims=True))
            stats.append((m_old, m_new, jnp.exp2(s_t - m_new).astype(BF16)))
        pvs = [_mm(vaug_ref[kv_slot, :, pl.ds(pl.multiple_of(kt * tk, tk), tk)], p)
               for (_, kv_slot, kt, _), (_, _, p) in zip(items, stats)]
        for (chain, _, _, _), (m_old, m_new, _), pv in zip(items, stats, pvs):
            acc_ref[chain] = jnp.exp2(m_old - m_new) * acc_ref[chain] + pv
            m_ref[chain] = m_new

    def run(tiles):
        for n, (chain, kv_slot, kt, q, bias) in enumerate(tiles):
            s_ref[n] = scores(kv_slot, kt, q) if bias is None else scores(kv_slot, kt, q) + bias
        items = [(chain, kv_slot, kt, n) for n, (chain, kv_slot, kt, q, bias) in enumerate(tiles)]
        for n in range(0, len(items), 2):
            absorb(items[n:n + 2])

    def finish(chain):
        acc = acc_ref[chain]
        return acc[0:hd] / acc[hd:hd + 1]

    lane_q = lax.broadcasted_iota(jnp.int32, (tq, LANE), 1)
    lane_c = lax.broadcasted_iota(jnp.int32, (ncp, LANE), 1)
    vc_t = vcmp_ref[0].T
    j_sub = lax.broadcasted_iota(jnp.int32, (SEL_ROWS, tq), 0)
    j8 = lax.broadcasted_iota(jnp.int32, (SUBLANE, tq), 0)
    ngrp = SEL_ROWS // SUBLANE
    groups = range(NSA_KV_GROUPS)

    def prepare(qt):
        t0 = (2 * j + qt) * tq
        qv = q_ref[0, qt * tq:(qt + 1) * tq, :].astype(F32)
        t_lane = t0 + (lax.broadcasted_iota(jnp.int32, (ncp, nq), 1) & (tq - 1))
        cmask = (lax.broadcasted_iota(jnp.int32, (ncp, nq), 0) * CMP_STRIDE + (CMP_BLOCK - 1)) <= t_lane
        cur = jnp.right_shift(t0 + lax.broadcasted_iota(jnp.int32, (SEL_ROWS, tq), 1),
                              int(math.log2(SLC_BLOCK)))
        bonus = jnp.where((j_sub == 0) | (j_sub == cur) | (j_sub == cur - 1), FORCE_BONUS, 0.0)
        valid = j_sub <= cur
        q_plain, q_slc, o_cmp = [], [], []
        for g in groups:
            heads = []
            for r in range(rep):
                hidx = g * rep + r
                c128 = qv[:, (hidx // 2) * LANE:(hidx // 2 + 1) * LANE]
                if hidx % 2:
                    c128 = pltpu.roll(c128, hd, 1)
                heads.append(jnp.where(lane_q < hd, c128 * (LOG2E * hd ** -0.5), 0.0))
            q_plain.append(jnp.concatenate(heads, axis=0).astype(BF16))

            kc = kcmp_ref[0]
            if g:
                kc = pltpu.roll(kc, hd, 1)
            kc = jnp.where(lane_c < hd, kc, 0.0).astype(BF16)
            s_t = jnp.where(cmask, _mm_nt(kc, q_plain[g]), NEG)
            e = jnp.exp2(s_t - jnp.max(s_t, axis=0, keepdims=True))
            p = jnp.where(cmask, e / jnp.sum(e, axis=0, keepdims=True), 0.0).astype(BF16)
            o_cmp.append(_mm(vc_t[g * hd:(g + 1) * hd].astype(BF16), p))
            imp4 = _mm(ovl_ref[...], p)
            imp = imp4[:, 0:tq]
            for r in range(1, rep):
                imp = imp + imp4[:, r * tq:(r + 1) * tq]

            score = jnp.where(valid, imp + bonus, NEG)
            grp = [score[SUBLANE * a:SUBLANE * (a + 1)] for a in range(ngrp)]
            rank = [jnp.zeros((SUBLANE, tq), F32) for _ in range(ngrp)]
            for k in range(SEL_ROWS):
                sk = jnp.broadcast_to(score[k:k + 1, :], (SUBLANE, tq))
                for a in range(ngrp):
                    if k < SUBLANE * a:
                        inc = jnp.where(sk >= grp[a], 1.0, 0.0)
                    elif k >= SUBLANE * (a + 1):
                        inc = jnp.where(sk > grp[a], 1.0, 0.0)
                    else:
                        inc = jnp.where(j8 > k - SUBLANE * a, jnp.where(sk >= grp[a], 1.0, 0.0),
                                        jnp.where(sk > grp[a], 1.0, 0.0))
                    rank[a] = rank[a] + inc
            bias_rows = [jnp.where(rk < ntop, 0.0, NEG) for rk in rank]
            b128 = jnp.concatenate([jnp.zeros((hd, tq), F32)] + bias_rows
                                   + [jnp.zeros((LANE - hd - SEL_ROWS, tq), F32)], axis=0)
            bq = b128.T
            q_slc.append(jnp.concatenate([h + bq for h in heads], axis=0).astype(BF16))
        return q_plain, q_slc, o_cmp

    prep = [prepare(qt) for qt in range(2)]
    q_plain = [p[0] for p in prep]
    q_slc = [p[1] for p in prep]

    chain = lambda br, qt, g: (br * 2 + qt) * NSA_KV_GROUPS + g
    m_ref[...] = jnp.full(m_ref.shape, NEG, F32)
    acc_ref[...] = jnp.zeros(acc_ref.shape, F32)
    tiled = lambda kind: jnp.concatenate([bias_ref[kind]] * rep, axis=1)

    def slc_step(t, carry):
        run([(chain(0, qt, g), g, 2 * t + u, q_slc[qt][g], None)
             for u in range(2) for qt in range(2) for g in groups])
        return carry

    lax.fori_loop(0, j, slc_step, 0)
    causal = tiled(BIAS_CAUSAL)
    run([(chain(0, 0, g), g, 2 * j, q_slc[0][g], causal) for g in groups]
        + [(chain(0, 1, g), g, 2 * j, q_slc[1][g], None) for g in groups]
        + [(chain(0, 1, g), g, 2 * j + 1, q_slc[1][g], causal) for g in groups])
    for qt in range(2):
        tiles = []
        for step, kind in enumerate((BIAS_WINLO, BIAS_ZERO, BIAS_CAUSAL)):
            kt = 2 * j + qt - 2 + step
            if step == 2:
                bias = causal
            elif step == 1 and qt == 1:
                bias = None
            else:
                bias = tiled(jnp.where(kt >= 0, kind, BIAS_NONE))
            tiles += [(chain(1, qt, g), 2 + g, jnp.maximum(kt, 0), q_plain[qt][g], bias) for g in groups]
        run(tiles)

    for qt in range(2):
        gates_t = jax.nn.sigmoid(gl_ref[0, qt * tq:(qt + 1) * tq, :]).T
        o_cmp = prep[qt][2]
        outs_t = []
        for g in groups:
            o_slc = finish(chain(0, qt, g))
            o_win = finish(chain(1, qt, g))
            for r in range(rep):
                c = (g * rep + r) * 3
                sl = slice(r * tq, (r + 1) * tq)
                outs_t.append(gates_t[c:c + 1] * o_cmp[g][:, sl] + gates_t[c + 1:c + 2] * o_slc[:, sl]
                              + gates_t[c + 2:c + 3] * o_win[:, sl])
        o_ref[0, qt * tq:(qt + 1) * tq, :] = _rms(jnp.concatenate(outs_t, axis=0).T, nw_ref[...]).astype(o_ref.dtype)


def _nsa(q3, kcmp, vcmp, kvs3, kvw3, gl3, consts, nw):
    b, s, _ = q3.shape
    tq = ATT_TQ
    ncp = kcmp.shape[1]
    row = lambda w: pl.BlockSpec((1, 2 * tq, w), lambda bi, i: (bi, i, 0))
    per_b = lambda n, w: pl.BlockSpec((1, n, w), lambda bi, i: (bi, 0, 0))
    return pl.pallas_call(
        functools.partial(_nsa_body, seq=s),
        grid=(b, s // (2 * tq)),
        in_specs=[row(NSA_WIDTH), per_b(ncp, KV_WIDTH), per_b(ncp, KV_WIDTH),
                  per_b(s, 2 * KV_WIDTH), per_b(s, 2 * KV_WIDTH), row(LANE)]
                 + [_full(c.shape) for c in consts] + [_full((1, NSA_WIDTH))],
        out_specs=row(NSA_WIDTH),
        out_shape=jax.ShapeDtypeStruct((b, s, NSA_WIDTH), BF16),
        scratch_shapes=[pltpu.VMEM((2 * NSA_KV_GROUPS, s, LANE), BF16),
                        pltpu.VMEM((2 * NSA_KV_GROUPS, V_ROWS, s), BF16),
                        pltpu.VMEM((4 * NSA_KV_GROUPS, V_ROWS, NSA_REP * tq), F32),
                        pltpu.VMEM((4 * NSA_KV_GROUPS, 1, NSA_REP * tq), F32),
                        pltpu.VMEM((4 * NSA_KV_GROUPS, ATT_TK, NSA_REP * tq), F32)],
        compiler_params=_params("parallel", "arbitrary"),
        name="nsa_attn",
    )(q3, kcmp, vcmp, kvs3, kvw3, gl3, *consts, nw)


def _shifted_rows(x, halo, sh):
    r = pltpu.roll(x, sh, 0)
    row = lax.broadcasted_iota(jnp.int32, (HALO, x.shape[1]), 0)
    top = jnp.where(row < sh, pltpu.roll(halo, sh, 0), r[0:HALO])
    return jnp.concatenate([top, r[HALO:]], axis=0)


def _ssd_body(z_ref, xbc_ref, dt_ref, dtt_ref, cw_ref, cb_ref, dtb_row_ref, dtb_col_ref,
              alog_row_ref, alog_col_ref, d_ref, nw_ref, y_ref, state_ref, halo_ref):
    L = SSD_CHUNK
    hd, nst = SSD_HEAD_DIM, SSD_STATE
    rep = SSD_HEADS // SSD_GROUPS
    gw = rep * hd

    @pl.when(pl.program_id(1) == 0)
    def _():
        state_ref[...] = jnp.zeros_like(state_ref)
        halo_ref[...] = jnp.zeros_like(halo_ref)

    xr = xbc_ref[0]
    halo = halo_ref[...]
    conv = cb_ref[...] + cw_ref[SSD_CONV - 1:SSD_CONV, :] * xr
    for sh in range(1, SSD_CONV):
        conv = conv + cw_ref[SSD_CONV - 1 - sh:SSD_CONV - sh, :] * _shifted_rows(xr, halo, sh)
    halo_ref[...] = xr[L - HALO:L, :]
    xc = _silu(conv)
    xs = xc[:, :SSD_WIDTH]
    bm = xc[:, SSD_WIDTH:SSD_WIDTH + SSD_GROUPS * nst]
    cm = xc[:, SSD_WIDTH + SSD_GROUPS * nst:]

    dt_col = jax.nn.softplus(dt_ref[0] + dtb_row_ref[...])
    a_col = dt_col * (-jnp.exp(alog_row_ref[...]))
    i0 = lax.broadcasted_iota(jnp.int32, (L, L), 0)
    i1 = lax.broadcasted_iota(jnp.int32, (L, L), 1)
    tri = i0 >= i1
    acs_col = jnp.dot(jnp.where(tri, 1.0, 0.0), a_col, precision=lax.Precision.HIGHEST,
                      preferred_element_type=F32)
    dt_row = jax.nn.softplus(dtt_ref[0] + dtb_col_ref[...])
    a_row = dt_row * (-jnp.exp(alog_col_ref[...]))
    acs_row = jnp.dot(a_row, jnp.where(i0 <= i1, 1.0, 0.0), precision=lax.Precision.HIGHEST,
                      preferred_element_type=F32)
    acs_last = acs_col[L - 1:L, :]

    ys = []
    for g in range(SSD_GROUPS):
        bg = bm[:, g * nst:(g + 1) * nst].astype(BF16)
        cg = cm[:, g * nst:(g + 1) * nst].astype(BF16)
        cb = _mm_nt(cg, bg)
        state = state_ref[g]
        y_off = _mm(cg, state.astype(BF16))
        sx, cdec = [], []
        for r in range(rep):
            h = g * rep + r
            col = acs_col[:, h:h + 1]
            seg = col - acs_row[h:h + 1, :]
            xh = xs[:, h * hd:(h + 1) * hd]
            xdt = xh * dt_col[:, h:h + 1]
            y = _mm(jnp.where(tri, cb * jnp.exp(seg), 0.0).astype(BF16), xdt.astype(BF16))
            y = y + jnp.exp(col) * y_off[:, r * hd:(r + 1) * hd] + d_ref[:, h * hd:(h + 1) * hd] * xh
            ys.append(y)
            last = acs_last[:, h:h + 1]
            sx.append(xdt * jnp.exp(last - col))
            cdec.append(jnp.broadcast_to(jnp.exp(last), (1, hd)))
        sxg = jnp.concatenate(sx, axis=1).astype(BF16)
        new = lax.dot_general(bg, sxg, _TN, preferred_element_type=F32)
        state_ref[g] = state * jnp.concatenate(cdec, axis=1) + new
    y = jnp.concatenate(ys, axis=1) * _silu(z_ref[0])
    outs = []
    for g in range(SSD_GROUPS):
        outs.append(_rms(y[:, g * gw:(g + 1) * gw], nw_ref[:, g * gw:(g + 1) * gw]))
    y_ref[0] = jnp.concatenate(outs, axis=1).astype(y_ref.dtype)


def _ssd(z3, xbc3, dt3, dtt3, cw, cb, dtb_row, dtb_col, alog_row, alog_col, d_exp, nw):
    b, s, _ = z3.shape
    L = SSD_CHUNK
    row = lambda w: pl.BlockSpec((1, L, w), lambda bi, c: (bi, c, 0))
    small = [cw, cb, dtb_row, dtb_col, alog_row, alog_col, d_exp, nw]
    return pl.pallas_call(
        _ssd_body,
        grid=(b, s // L),
        in_specs=[row(SSD_WIDTH), row(SSD_XBC), row(LANE),
                  pl.BlockSpec((1, SSD_HEADS, L), lambda bi, c: (bi, 0, c))]
                 + [_full(a.shape) for a in small],
        out_specs=row(SSD_WIDTH),
        out_shape=jax.ShapeDtypeStruct((b, s, SSD_WIDTH), BF16),
        scratch_shapes=[pltpu.VMEM((SSD_GROUPS, SSD_STATE, SSD_WIDTH // SSD_GROUPS), F32),
                        pltpu.VMEM((HALO, SSD_XBC), F32)],
        compiler_params=_params("parallel", "arbitrary"),
        name="ssd",
    )(z3, xbc3, dt3, dtt3, *small)


def _ffn_body(x_ref, a_ref, s_ref, wa_ref, ws_ref, nw_ref, wg_ref, wu_ref, cw_ref, cb_ref, wd_ref, fw_ref,
              o_ref, halo_ref, act_ref, *, final):
    tm = x_ref.shape[1]
    nf = wg_ref.shape[1] // FFN_TF

    @pl.when(pl.program_id(1) == 0)
    def _():
        halo_ref[...] = jnp.zeros_like(halo_ref)

    x = x_ref[0] + _mm(a_ref[0], wa_ref[...]) + _mm(s_ref[0], ws_ref[...])
    h = _rms(x, nw_ref[...]).astype(BF16)
    for f in range(nf):
        gp = _mm(h, wg_ref[:, f * FFN_TF:(f + 1) * FFN_TF])
        up = _mm(h, wu_ref[:, f * FFN_TF:(f + 1) * FFN_TF])
        halo = halo_ref[f]
        cw = cw_ref[f]
        gate = cb_ref[f] + cw[FFN_CONV - 1:FFN_CONV, :] * gp
        for sh in range(1, FFN_CONV):
            gate = gate + cw[FFN_CONV - 1 - sh:FFN_CONV - sh, :] * _shifted_rows(gp, halo, sh)
        halo_ref[f] = gp[tm - HALO:tm, :]
        act_ref[:, f * FFN_TF:(f + 1) * FFN_TF] = (_silu(gate) * up).astype(BF16)
    y = x + _mm(act_ref[...], wd_ref[...])
    o_ref[0] = _rms(y, fw_ref[...]) if final else y


def _ffn(x3, a3, s3, wa, ws, nw, wg, wu, cw, cb, wd, fw, final, tm=FFN_TM):
    b, s, _ = x3.shape
    nf = wg.shape[1] // FFN_TF
    row = lambda w: pl.BlockSpec((1, tm, w), lambda bi, i: (bi, i, 0))
    once = lambda a: pl.BlockSpec(a.shape, lambda *_: (0,) * a.ndim, pipeline_mode=pl.Buffered(1))
    return pl.pallas_call(
        functools.partial(_ffn_body, final=final),
        grid=(b, s // tm),
        in_specs=[row(D_MODEL), row(NSA_WIDTH), row(SSD_WIDTH), once(wa), once(ws), _full(nw.shape),
                  once(wg), once(wu), _full(cw.shape), _full(cb.shape), once(wd), _full(fw.shape)],
        out_specs=row(D_MODEL),
        out_shape=jax.ShapeDtypeStruct((b, s, D_MODEL), F32),
        scratch_shapes=[pltpu.VMEM((nf, HALO, FFN_TF), F32), pltpu.VMEM((tm, D_FF), BF16)],
        compiler_params=_params("parallel", "arbitrary"),
        name="conv_ffn",
    )(x3, a3, s3, wa, ws, nw, wg, wu, cw, cb, wd, fw)


def _pad_lanes(v):
    return jnp.pad(v, (0, LANE - v.shape[0]))[None, :]


def kernel(x, norm_mix_w, w_in, cmp_pos_k, cmp_w1_k, cmp_b1_k, cmp_w2_k, cmp_pos_v, cmp_w1_v,
           cmp_b1_v, cmp_w2_v, nsa_norm_w, ssd_conv_w, ssd_conv_b, ssd_dt_bias, ssd_A_log, ssd_D,
           ssd_norm_w, w_out, norm_ffn_w, w_gate, w_up, ffn_conv_w, ffn_conv_b, w_down, norm_final_w):
    b, s, d = x.shape
    depth = w_in.shape[0]
    t = b * s
    nf = D_FF // FFN_TF
    nsa_consts = _nsa_consts(s)
    x3 = x
    for i in range(depth):
        x2 = x3.reshape(t, d)
        q, kc, vc, kvs, kvw, gl, dt, z, xbc = _in_proj(x2, norm_mix_w[i][None, :], _pack_w_in(w_in[i]))
        ncp = s // CMP_STRIDE
        kcmp, vcmp = _compress(
            kc.reshape(b, s, KV_WIDTH), vc.reshape(b, s, KV_WIDTH),
            _pack_compress(cmp_pos_k[i], cmp_w1_k[i], cmp_b1_k[i], cmp_w2_k[i]),
            _pack_compress(cmp_pos_v[i], cmp_w1_v[i], cmp_b1_v[i], cmp_w2_v[i]))
        o_attn = _nsa(q.reshape(b, s, -1), kcmp, vcmp, kvs.reshape(b, s, -1), kvw.reshape(b, s, -1),
                      gl.reshape(b, s, -1), nsa_consts, nsa_norm_w[i][None, :])
        dt3 = dt.reshape(b, s, LANE)
        dtt3 = jnp.swapaxes(dt3[:, :, :SSD_HEADS], 1, 2)
        o_ssd = _ssd(z.reshape(b, s, -1), xbc.reshape(b, s, -1), dt3, dtt3,
                     ssd_conv_w[i], ssd_conv_b[i][None, :],
                     _pad_lanes(ssd_dt_bias[i]), ssd_dt_bias[i][:, None],
                     _pad_lanes(ssd_A_log[i]), ssd_A_log[i][:, None],
                     jnp.repeat(ssd_D[i], SSD_HEAD_DIM)[None, :], ssd_norm_w[i][None, :])
        wo = w_out[i].astype(BF16)
        x3 = _ffn(x3, o_attn, o_ssd, wo[:NSA_WIDTH], wo[NSA_WIDTH:], norm_ffn_w[i][None, :],
                  w_gate[i].astype(BF16), w_up[i].astype(BF16),
                  ffn_conv_w[i].reshape(FFN_CONV, nf, FFN_TF).transpose(1, 0, 2),
                  ffn_conv_b[i].reshape(nf, 1, FFN_TF),
                  w_down[i].astype(BF16), norm_final_w[None, :], final=(i == depth - 1))
    return x3
```
